```python
import math
import jax, jax.numpy as jnp
from jax import lax
import numpy as np

D_MODEL = 1024
BATCH = 8
SEQ = 2048
DEPTH = 1

MLA_HEADS = 8
QK_NOPE_DIM = 64
QK_ROPE_DIM = 32
QK_HEAD_DIM = QK_NOPE_DIM + QK_ROPE_DIM
V_HEAD_DIM = 64
Q_LORA_RANK = 256
KV_LORA_RANK = 128
ROPE_THETA = 10000.0
Q_BLOCK = 128
HYENA_WIDTH = 512
HYENA_GROUPS = 8
FILTER_EMB = 33
FILTER_ORDER = 64
FAST_DECAY_PCT = 0.3
SLOW_DECAY_PCT = 1.5
DECAY_TARGET = 1e-2
PEER_HEADS = 8
N_KEYS = 128
N_EXPERTS = N_KEYS * N_KEYS
PEER_KEY_DIM = 256
PEER_TOPK = 16
PEER_TOKEN_BLOCK = 128
GATE_WIDTH = 2 * D_MODEL
IN_SPLITS = (Q_LORA_RANK, KV_LORA_RANK + QK_ROPE_DIM, 3 * HYENA_WIDTH, GATE_WIDTH)
IN_COLS = sum(IN_SPLITS)
EPS = 1e-6

kernel_name = 'hybrid_mla_hyena_peer_encoder'


def rms_norm(x, g):
    xf = x.astype(jnp.float32)
    y = xf * lax.rsqrt(jnp.mean(xf * xf, axis=-1, keepdims=True) + EPS)
    return (y * g.astype(jnp.float32)).astype(x.dtype)


def rotary(x):
    S = x.shape[1]
    half = QK_ROPE_DIM // 2
    inv_freq = ROPE_THETA ** (-jnp.arange(half, dtype=jnp.float32) / half)
    ang = jnp.arange(S, dtype=jnp.float32)[:, None] * inv_freq[None, :]
    cos = jnp.cos(ang)[None, :, None, :]
    sin = jnp.sin(ang)[None, :, None, :]
    x1 = x[..., :half].astype(jnp.float32)
    x2 = x[..., half:].astype(jnp.float32)
    out = jnp.concatenate([x1 * cos - x2 * sin, x2 * cos + x1 * sin], axis=-1)
    return out.astype(x.dtype)


def block_attention(q, k, v):
    B, S, H, Dh = q.shape
    nb = S // Q_BLOCK
    scale = QK_HEAD_DIM ** -0.5
    qb = q.reshape(B, nb, Q_BLOCK, H, Dh).transpose(1, 0, 2, 3, 4)

    def one_block(qi):
        s = jnp.einsum('bqhd,bkhd->bhqk', qi, k, preferred_element_type=jnp.float32) * scale
        p = jax.nn.softmax(s, axis=-1).astype(v.dtype)
        return jnp.einsum('bhqk,bkhd->bqhd', p, v)

    o = lax.map(one_block, qb)
    return o.transpose(1, 0, 2, 3, 4).reshape(B, S, H * V_HEAD_DIM)


def mla_branch(c_q, ckv_pe, q_a_norm, w_uq, kv_a_norm, w_ukv, q_norm, k_norm):
    B, S, _ = c_q.shape
    c_q = rms_norm(c_q, q_a_norm)
    q = (c_q @ w_uq).reshape(B, S, MLA_HEADS, QK_HEAD_DIM)
    c_kv = rms_norm(ckv_pe[..., :KV_LORA_RANK], kv_a_norm)
    k_pe = ckv_pe[..., KV_LORA_RANK:]
    kv = (c_kv @ w_ukv).reshape(B, S, MLA_HEADS, QK_NOPE_DIM + V_HEAD_DIM)
    k_nope, v = kv[..., :QK_NOPE_DIM], kv[..., QK_NOPE_DIM:]
    k = jnp.concatenate([k_nope, jnp.broadcast_to(k_pe[:, :, None, :], (B, S, MLA_HEADS, QK_ROPE_DIM))], axis=-1)
    q = rms_norm(q, q_norm)
    k = rms_norm(k, k_norm)
    q = jnp.concatenate([q[..., :QK_NOPE_DIM], rotary(q[..., QK_NOPE_DIM:])], axis=-1)
    k = jnp.concatenate([k[..., :QK_NOPE_DIM], rotary(k[..., QK_NOPE_DIM:])], axis=-1)
    return block_attention(q, k, v)


def short_conv(u, w, b):
    S = u.shape[1]
    up = jnp.pad(u, ((0, 0), (1, 1), (0, 0)))
    return up[:, :S] * w[0] + up[:, 1:S + 1] * w[1] + up[:, 2:] * w[2] + b


def implicit_filters(L, w1, b1, w2, b2, w3, b3, w4, b4, freq):
    f32 = jnp.float32
    t = jnp.linspace(0.0, 1.0, L, dtype=f32)[:, None]
    bands = (FILTER_EMB - 1) // 2
    w = 2.0 * math.pi * jnp.arange(L, dtype=f32)[:, None] / L
    f = jnp.linspace(1e-4, bands - 1, bands, dtype=f32)[None, :]
    z = jnp.concatenate([t, jnp.cos(f * w), -jnp.sin(f * w)], axis=-1)
    fr = freq.astype(f32)
    h = jnp.sin(fr * (z @ w1.astype(f32) + b1.astype(f32)))
    h = jnp.sin(fr * (h @ w2.astype(f32) + b2.astype(f32)))
    h = jnp.sin(fr * (h @ w3.astype(f32) + b3.astype(f32)))
    h = h @ w4.astype(f32) + b4.astype(f32)
    min_decay = math.log(DECAY_TARGET) / SLOW_DECAY_PCT
    max_decay = math.log(DECAY_TARGET) / FAST_DECAY_PCT
    deltas = jnp.abs(jnp.linspace(min_decay, max_decay, HYENA_WIDTH, dtype=f32))
    decay = jnp.exp(-t * deltas[None, :])
    h = h.reshape(L, 2, HYENA_WIDTH) * decay[:, None, :]
    h_fwd, h_bwd = h[:, 0], h[:, 1]
    return jnp.concatenate([h_fwd, jnp.zeros((1, HYENA_WIDTH), f32), h_bwd[1:][::-1]], axis=0)


def long_conv(z, k_two, bias):
    L = z.shape[1]
    zf = jnp.fft.rfft(z.astype(jnp.float32), n=2 * L, axis=1)
    kf = jnp.fft.rfft(k_two, n=2 * L, axis=0)
    y = jnp.fft.irfft(zf * kf[None], n=2 * L, axis=1)[:, :L]
    return (y + z.astype(jnp.float32) * bias.astype(jnp.float32)).astype(z.dtype)


def hyena_branch(u, conv_w, conv_b, w1, b1, w2, b2, w3, b3, w4, b4, freq, bias):
    L = u.shape[1]
    uc = short_conv(u, conv_w, conv_b)
    x0, x1, v = jnp.split(uc, 3, axis=-1)
    k_two = implicit_filters(L, w1, b1, w2, b2, w3, b3, w4, b4, freq)
    z = long_conv(v * x1, k_two, bias)
    return x0 * z


def peer(xn, w_q, keys1, keys2, expert_u, expert_v):
    B, S, D = xn.shape
    nb = (B * S) // PEER_TOKEN_BLOCK
    xt = xn.reshape(nb, PEER_TOKEN_BLOCK, D)
    half = PEER_KEY_DIM // 2

    def one_block(xb):
        q = (xb @ w_q).reshape(PEER_TOKEN_BLOCK, PEER_HEADS, 2, half)
        s1 = jnp.einsum('thd,hkd->thk', q[:, :, 0], keys1, preferred_element_type=jnp.float32)
        s2 = jnp.einsum('thd,hkd->thk', q[:, :, 1], keys2, preferred_element_type=jnp.float32)
        v1, i1 = lax.top_k(s1, PEER_TOPK)
        v2, i2 = lax.top_k(s2, PEER_TOPK)
        cand = (v1[..., :, None] + v2[..., None, :]).reshape(PEER_TOKEN_BLOCK, PEER_HEADS, PEER_TOPK * PEER_TOPK)
        cand_idx = (i1[..., :, None] * N_KEYS + i2[..., None, :]).reshape(PEER_TOKEN_BLOCK, PEER_HEADS, PEER_TOPK * PEER_TOPK)
        top_s, pos = lax.top_k(cand, PEER_TOPK)
        idx = jnp.take_along_axis(cand_idx, pos, axis=-1)
        g = jax.nn.softmax(top_s, axis=-1)
        u = expert_u[idx]
        act = jax.nn.gelu(jnp.einsum('thkd,td->thk', u, xb), approximate=False)
        wgt = (g * act).astype(xb.dtype)
        vv = expert_v[idx]
        return jnp.einsum('thk,thkd->td', wgt, vv)

    y = lax.map(one_block, xt)
    return y.reshape(B, S, D)


def setup_inputs(seed: int = 0) -> dict:
    key = jax.random.key(seed)
    ks = jax.random.split(key, 32)
    nrm = lambda k, shape, s: jax.random.normal(k, shape, jnp.float32) * s
    gain = lambda k, n: 1.0 + 0.02 * jax.random.normal(k, (DEPTH, n), jnp.float32)
    Ld = DEPTH
    return {
        'x': nrm(ks[0], (BATCH, SEQ, D_MODEL), 1.0),
        'attn_norm': gain(ks[1], D_MODEL),
        'w_in': nrm(ks[2], (Ld, D_MODEL, IN_COLS), D_MODEL ** -0.5),
        'b_gate': nrm(ks[3], (Ld, GATE_WIDTH), 0.02),
        'q_a_norm': gain(ks[4], Q_LORA_RANK),
        'w_uq': nrm(ks[5], (Ld, Q_LORA_RANK, MLA_HEADS * QK_HEAD_DIM), Q_LORA_RANK ** -0.5),
        'kv_a_norm': gain(ks[6], KV_LORA_RANK),
        'w_ukv': nrm(ks[7], (Ld, KV_LORA_RANK, MLA_HEADS * (QK_NOPE_DIM + V_HEAD_DIM)), KV_LORA_RANK ** -0.5),
        'q_norm': gain(ks[8], QK_HEAD_DIM),
        'k_norm': gain(ks[9], QK_HEAD_DIM),
        'w_o_attn': nrm(ks[10], (Ld, MLA_HEADS * V_HEAD_DIM, D_MODEL), (MLA_HEADS * V_HEAD_DIM) ** -0.5),
        'hyena_conv_w': nrm(ks[11], (Ld, 3, 3 * HYENA_WIDTH), 3 ** -0.5),
        'hyena_conv_b': nrm(ks[12], (Ld, 3 * HYENA_WIDTH), 0.02),
        'filt_w1': nrm(ks[13], (Ld, FILTER_EMB, FILTER_ORDER), FILTER_EMB ** -0.5),
        'filt_b1': nrm(ks[14], (Ld, FILTER_ORDER), 0.1),
        'filt_w2': nrm(ks[15], (Ld, FILTER_ORDER, FILTER_ORDER), FILTER_ORDER ** -0.5),
        'filt_b2': nrm(ks[16], (Ld, FILTER_ORDER), 0.1),
        'filt_w3': nrm(ks[17], (Ld, FILTER_ORDER, FILTER_ORDER), FILTER_ORDER ** -0.5),
        'filt_b3': nrm(ks[18], (Ld, FILTER_ORDER), 0.1),
        'filt_w4': nrm(ks[19], (Ld, FILTER_ORDER, 2 * HYENA_WIDTH), 0.125 * FILTER_ORDER ** -0.5),
        'filt_b4': nrm(ks[20], (Ld, 2 * HYENA_WIDTH), 0.01),
        'filt_freq': gain(ks[21], FILTER_ORDER),
        'hyena_bias': nrm(ks[22], (Ld, HYENA_WIDTH), 1.0),
        'w_o_hyena': nrm(ks[23], (Ld, HYENA_WIDTH, D_MODEL), HYENA_WIDTH ** -0.5),
        'w_out': nrm(ks[24], (Ld, D_MODEL, D_MODEL), D_MODEL ** -0.5),
        'ffn_norm': gain(ks[25], D_MODEL),
        'peer_w_q': nrm(ks[26], (Ld, D_MODEL, PEER_HEADS * PEER_KEY_DIM), D_MODEL ** -0.5),
        'peer_keys1': nrm(ks[27], (Ld, PEER_HEADS, N_KEYS, PEER_KEY_DIM // 2), (PEER_KEY_DIM // 2) ** -0.5),
        'peer_keys2': nrm(ks[28], (Ld, PEER_HEADS, N_KEYS, PEER_KEY_DIM // 2), (PEER_KEY_DIM // 2) ** -0.5),
        'expert_u': nrm(ks[29], (Ld, N_EXPERTS, D_MODEL), D_MODEL ** -0.5),
        'expert_v': nrm(ks[30], (Ld, N_EXPERTS, D_MODEL), (PEER_HEADS * PEER_TOPK) ** -0.5),
    }


def reference(x, attn_norm, w_in, b_gate, q_a_norm, w_uq, kv_a_norm, w_ukv, q_norm, k_norm, w_o_attn,
              hyena_conv_w, hyena_conv_b, filt_w1, filt_b1, filt_w2, filt_b2, filt_w3, filt_b3, filt_w4, filt_b4,
              filt_freq, hyena_bias, w_o_hyena, w_out, ffn_norm, peer_w_q, peer_keys1, peer_keys2, expert_u, expert_v):
    offs = list(np.cumsum(IN_SPLITS)[:-1])
    h = x
    for layer in range(DEPTH):
        xn = rms_norm(h, attn_norm[layer])
        proj = xn @ w_in[layer]
        c_q, ckv_pe, u_hy, gate_logits = jnp.split(proj, offs, axis=-1)
        gates = jax.nn.sigmoid((gate_logits + b_gate[layer]).astype(jnp.float32))
        g_attn, g_hy = gates[..., :D_MODEL], gates[..., D_MODEL:]
        a = mla_branch(c_q, ckv_pe, q_a_norm[layer], w_uq[layer], kv_a_norm[layer], w_ukv[layer],
                       q_norm[layer], k_norm[layer]) @ w_o_attn[layer]
        y_hy = hyena_branch(u_hy, hyena_conv_w[layer], hyena_conv_b[layer], filt_w1[layer], filt_b1[layer],
                            filt_w2[layer], filt_b2[layer], filt_w3[layer], filt_b3[layer], filt_w4[layer],
                            filt_b4[layer], filt_freq[layer], hyena_bias[layer]) @ w_o_hyena[layer]
        merged = (g_attn * a + g_hy * y_hy).astype(h.dtype)
        h = h + merged @ w_out[layer]
        hn = rms_norm(h, ffn_norm[layer])
        h = h + peer(hn, peer_w_q[layer], peer_keys1[layer], peer_keys2[layer], expert_u[layer], expert_v[layer])
    return h
```

```python
import functools
import math

import numpy as np
import jax
import jax.numpy as jnp
from jax import lax
from jax.experimental import pallas as pl
from jax.experimental.pallas import tpu as pltpu

F32 = jnp.float32
BF16 = jnp.bfloat16

D_MODEL = 1024
SEQ = 2048
MLA_HEADS = 8
QK_NOPE_DIM = 64
QK_ROPE_DIM = 32
QK_HEAD_DIM = QK_NOPE_DIM + QK_ROPE_DIM
V_HEAD_DIM = 64
Q_LORA_RANK = 256
KV_LORA_RANK = 128
ROPE_THETA = 10000.0
HYENA_WIDTH = 512
FILTER_EMB = 33
FILTER_ORDER = 64
FAST_DECAY_PCT = 0.3
SLOW_DECAY_PCT = 1.5
DECAY_TARGET = 1e-2
PEER_HEADS = 8
N_KEYS = 128
N_EXPERTS = N_KEYS * N_KEYS
PEER_KEY_DIM = 256
PEER_TOPK = 16
EPS = 1e-6

LANES = 128
HEAD_SLOT = LANES
FFT_N = 2 * SEQ
PROJ_COLS = 4096
COL_CQ, COL_CKV, COL_KPE, COL_HY, COL_GATE = 0, 256, 384, 512, 2048
KPE_LANE = QK_NOPE_DIM
VMEM_LIMIT = 56 * 1024 * 1024


def _cparams(sem):
    return pltpu.CompilerParams(dimension_semantics=sem, vmem_limit_bytes=VMEM_LIMIT)


def _dot(a, b):
    return jnp.dot(a, b, preferred_element_type=F32)


def _dot_nt(a, b):
    return lax.dot_general(a, b, (((1,), (1,)), ((), ())), preferred_element_type=F32)


def _rms(x, g):
    return x * lax.rsqrt(jnp.mean(x * x, axis=-1, keepdims=True) + EPS) * g


@functools.lru_cache(maxsize=None)
def _dft_tables():
    p = np.arange(FFT_N, dtype=np.int64)[:, None]
    s = np.arange(SEQ, dtype=np.int64)[None, :]
    f = np.where(p <= SEQ, p, p - SEQ)
    ang = (2.0 * np.pi / FFT_N) * ((f * s) % FFT_N).astype(np.float64)
    w = np.where(p <= SEQ, np.cos(ang), np.sin(ang)).astype(np.float32)
    return w, np.ascontiguousarray(w.T)


@functools.lru_cache(maxsize=None)
def _rotary_tables():
    half = QK_ROPE_DIM // 2
    inv_freq = ROPE_THETA ** (-np.arange(half, dtype=np.float64) / half)
    ang = np.arange(SEQ, dtype=np.float64)[:, None] * inv_freq[None, :]
    cos = np.ones((SEQ, HEAD_SLOT)); sa = np.zeros((SEQ, HEAD_SLOT)); sb = np.zeros((SEQ, HEAD_SLOT))
    lo, mid, hi = QK_NOPE_DIM, QK_NOPE_DIM + half, QK_HEAD_DIM
    cos[:, lo:mid] = np.cos(ang); cos[:, mid:hi] = np.cos(ang)
    sb[:, lo:mid] = -np.sin(ang)
    sa[:, mid:hi] = np.sin(ang)
    return tuple(jnp.asarray(t, dtype=F32) for t in (cos, sa, sb))


@functools.lru_cache(maxsize=None)
def _filter_consts():
    L = SEQ
    t = np.linspace(0.0, 1.0, L)[:, None]
    bands = (FILTER_EMB - 1) // 2
    w = 2.0 * math.pi * np.arange(L)[:, None] / L
    f = np.linspace(1e-4, bands - 1, bands)[None, :]
    z = np.concatenate([t, np.cos(f * w), -np.sin(f * w)], axis=-1)
    zp = np.zeros((L, LANES)); zp[:, :FILTER_EMB] = z
    min_decay = math.log(DECAY_TARGET) / SLOW_DECAY_PCT
    max_decay = math.log(DECAY_TARGET) / FAST_DECAY_PCT
    deltas = np.abs(np.linspace(min_decay, max_decay, HYENA_WIDTH))[None, :]
    return jnp.asarray(zp, dtype=F32), jnp.asarray(deltas, dtype=F32)


INPROJ_TM = 512
INPROJ_CW = 1024


def _inproj_kernel(x_ref, g_ref, w_ref, o_ref):
    xn = _rms(x_ref[...], g_ref[...]).astype(BF16)
    for c in range(PROJ_COLS // INPROJ_CW):
        sl = slice(c * INPROJ_CW, (c + 1) * INPROJ_CW)
        o_ref[:, sl] = _dot(xn, w_ref[:, sl]).astype(BF16)


def _inproj(x2, g, w_p):
    T = x2.shape[0]
    return pl.pallas_call(
        _inproj_kernel,
        grid=(T // INPROJ_TM,),
        in_specs=[pl.BlockSpec((INPROJ_TM, D_MODEL), lambda i: (i, 0)),
                  pl.BlockSpec((1, D_MODEL), lambda i: (0, 0)),
                  pl.BlockSpec((D_MODEL, PROJ_COLS), lambda i: (0, 0))],
        out_specs=pl.BlockSpec((INPROJ_TM, PROJ_COLS), lambda i: (i, 0)),
        out_shape=jax.ShapeDtypeStruct((T, PROJ_COLS), BF16),
        compiler_params=_cparams(("parallel",)),
        name="inproj",
    )(x2, g, w_p)


MLA_TM = 512
HW = MLA_HEADS * HEAD_SLOT


def _rope(x, cos, sa, sb):
    half = QK_ROPE_DIM // 2
    return x * cos + pltpu.roll(x, half, 1) * sa + pltpu.roll(x, HEAD_SLOT - half, 1) * sb


def _mlaprep_kernel(p_ref, qa_ref, kva_ref, wuq_ref, wuk_ref, wuv_ref, gq_ref, gk_ref, vone_ref,
                    cos_ref, sa_ref, sb_ref, q_ref, k_ref, v_ref):
    p = p_ref[...].astype(F32)
    cqn = _rms(p[:, COL_CQ:COL_CQ + Q_LORA_RANK], qa_ref[...]).astype(BF16)
    ckvn = _rms(p[:, COL_CKV:COL_CKV + KV_LORA_RANK], kva_ref[...]).astype(BF16)
    kpe = p[:, COL_KPE:COL_KPE + HEAD_SLOT]
    q = _dot(cqn, wuq_ref[...])
    kn = _dot(ckvn, wuk_ref[...])
    v_ref[...] = (_dot(ckvn, wuv_ref[...]) + vone_ref[...]).astype(BF16)
    cos, sa, sb = cos_ref[...], sa_ref[...], sb_ref[...]
    scale = QK_HEAD_DIM ** -0.5
    inv_d = 1.0 / QK_HEAD_DIM
    for h in range(MLA_HEADS):
        sl = slice(h * HEAD_SLOT, (h + 1) * HEAD_SLOT)
        qh = q[:, sl]
        qh = qh * lax.rsqrt(jnp.sum(qh * qh, axis=-1, keepdims=True) * inv_d + EPS) * gq_ref[...]
        q_ref[:, sl] = (_rope(qh, cos, sa, sb) * scale).astype(BF16)
        kh = kn[:, sl] + kpe
        kh = kh * lax.rsqrt(jnp.sum(kh * kh, axis=-1, keepdims=True) * inv_d + EPS) * gk_ref[...]
        k_ref[:, sl] = _rope(kh, cos, sa, sb).astype(BF16)


def _mlaprep(proj, qa, kva, wuq_p, wuk_p, wuv_p, gq, gk, vone):
    T = proj.shape[0]
    cos, sa, sb = _rotary_tables()
    nseq = SEQ // MLA_TM
    full = lambda shape: pl.BlockSpec(shape, lambda i: (0, 0))
    pos = pl.BlockSpec((MLA_TM, HEAD_SLOT), lambda i: (i % nseq, 0))
    out = pl.BlockSpec((MLA_TM, HW), lambda i: (i, 0))
    return pl.pallas_call(
        _mlaprep_kernel,
        grid=(T // MLA_TM,),
        in_specs=[pl.BlockSpec((MLA_TM, 512), lambda i: (i, 0)),
                  full((1, Q_LORA_RANK)), full((1, KV_LORA_RANK)),
                  full((Q_LORA_RANK, HW)), full((KV_LORA_RANK, HW)), full((KV_LORA_RANK, HW)),
                  full((1, HEAD_SLOT)), full((1, HEAD_SLOT)), full((1, HW)),
                  pos, pos, pos],
        out_specs=[out, out, out],
        out_shape=[jax.ShapeDtypeStruct((T, HW), BF16)] * 3,
        compiler_params=_cparams(("parallel",)),
        name="mlaprep",
    )(proj, qa, kva, wuq_p, wuk_p, wuv_p, gq, gk, vone, cos, sa, sb)


ATT_TQ = 512


def _attn_kernel(q_ref, k_ref, v_ref, o_ref):
    outs = []
    for hh in range(2):
        sl = slice(hh * HEAD_SLOT, (hh + 1) * HEAD_SLOT)
        s = _dot_nt(q_ref[0, :, sl], k_ref[0, :, sl])
        m = jnp.max(s, axis=-1, keepdims=True)
        p = jnp.exp(s - m).astype(BF16)
        pv = _dot(p, v_ref[0, :, sl])
        outs.append(pv[:, :V_HEAD_DIM] / pv[:, V_HEAD_DIM:V_HEAD_DIM + 1])
    o_ref[0] = jnp.concatenate(outs, axis=-1).astype(BF16)


def _attention(q, k, v, B):
    q3, k3, v3 = (a.reshape(B, SEQ, HW) for a in (q, k, v))
    pair = 2 * HEAD_SLOT
    return pl.pallas_call(
        _attn_kernel,
        grid=(B, MLA_HEADS // 2, SEQ // ATT_TQ),
        in_specs=[pl.BlockSpec((1, ATT_TQ, pair), lambda b, h, i: (b, i, h)),
                  pl.BlockSpec((1, SEQ, pair), lambda b, h, i: (b, 0, h)),
                  pl.BlockSpec((1, SEQ, pair), lambda b, h, i: (b, 0, h))],
        out_specs=pl.BlockSpec((1, ATT_TQ, 2 * V_HEAD_DIM), lambda b, h, i: (b, i, h)),
        out_shape=jax.ShapeDtypeStruct((B, SEQ, MLA_HEADS * V_HEAD_DIM), BF16),
        compiler_params=_cparams(("parallel", "parallel", "parallel")),
        name="attn",
    )(q3, k3, v3)


FILT_TL = 512


def _filter_kernel(z_ref, w1_ref, b1_ref, w2_ref, b2_ref, w3_ref, b3_ref, w4_ref, b4_ref, fr_ref,
                   dl_ref, o_ref):
    hp = functools.partial(jnp.dot, precision=lax.Precision.HIGHEST, preferred_element_type=F32)
    z = z_ref[...]
    fr = fr_ref[...]
    h = jnp.sin(fr * (hp(z, w1_ref[...]) + b1_ref[...]))
    h = jnp.sin(fr * (hp(h, w2_ref[...]) + b2_ref[...]))
    h = jnp.sin(fr * (hp(h, w3_ref[...]) + b3_ref[...]))
    h4 = hp(h, w4_ref[...]) + b4_ref[...]
    decay = jnp.exp(-z[:, 0:1] * dl_ref[...])
    hf = h4[:, :HYENA_WIDTH] * decay
    hb = h4[:, HYENA_WIDTH:] * decay
    row = pl.program_id(0) * FILT_TL + lax.broadcasted_iota(jnp.int32, hb.shape, 0)
    hb = jnp.where(row == 0, 0.0, hb)
    o_ref[:, :HYENA_WIDTH] = (hf + hb).astype(BF16)
    o_ref[:, HYENA_WIDTH:] = (hf - hb).astype(BF16)


def _pad2(a, r, c):
    return jnp.zeros((r, c), a.dtype).at[:a.shape[0], :a.shape[1]].set(a)


def _filter(w1, b1, w2, b2, w3, b3, w4, b4, freq):
    z, deltas = _filter_consts()
    P = LANES
    args = (z, _pad2(w1, P, P), _pad2(b1[None], 1, P), _pad2(w2, P, P), _pad2(b2[None], 1, P),
            _pad2(w3, P, P), _pad2(b3[None], 1, P), _pad2(w4, P, 2 * HYENA_WIDTH), b4[None],
            _pad2(freq[None], 1, P), deltas)
    full = lambda a: pl.BlockSpec(a.shape, lambda i: (0, 0))
    return pl.pallas_call(
        _filter_kernel,
        grid=(SEQ // FILT_TL,),
        in_specs=[pl.BlockSpec((FILT_TL, P), lambda i: (i, 0))] + [full(a) for a in args[1:]],
        out_specs=pl.BlockSpec((FILT_TL, 2 * HYENA_WIDTH), lambda i: (i, 0)),
        out_shape=jax.ShapeDtypeStruct((SEQ, 2 * HYENA_WIDTH), BF16),
        compiler_params=_cparams(("parallel",)),
        name="filt",
    )(*args)


SPEC_TR = 512


def _kspec_kernel(w_ref, h_ref, o_ref):
    r = _dot(w_ref[...], h_ref[...])
    row = pl.program_id(0) * SPEC_TR + lax.broadcasted_iota(jnp.int32, (SPEC_TR, HYENA_WIDTH), 0)
    o_ref[...] = jnp.where(row <= SEQ, r[:, :HYENA_WIDTH], r[:, HYENA_WIDTH:])


def _kspec(wf, hsd):
    return pl.pallas_call(
        _kspec_kernel,
        grid=(FFT_N // SPEC_TR,),
        in_specs=[pl.BlockSpec((SPEC_TR, SEQ), lambda i: (i, 0)),
                  pl.BlockSpec((SEQ, 2 * HYENA_WIDTH), lambda i: (0, 0))],
        out_specs=pl.BlockSpec((SPEC_TR, HYENA_WIDTH), lambda i: (i, 0)),
        out_shape=jax.ShapeDtypeStruct((FFT_N, HYENA_WIDTH), F32),
        compiler_params=_cparams(("parallel",)),
        name="kspec",
    )(wf, hsd)


def _short_conv(u, w, b):
    row = lax.broadcasted_iota(jnp.int32, u.shape, 0)
    prev = jnp.where(row == 0, 0.0, pltpu.roll(u, 1, 0))
    nxt = jnp.where(row == SEQ - 1, 0.0, pltpu.roll(u, SEQ - 1, 0))
    return prev * w[0:1] + u * w[1:2] + nxt * w[2:3] + b


def _hyfwd_kernel(x0_ref, x1_ref, v_ref, cw_ref, cb_ref, bias_ref, wre_ref, wim_ref, kre_ref, kim_ref,
                  y_ref, p1_ref, p2_ref, z_ref):
    r = pl.program_id(1)
    C = HYENA_WIDTH

    @pl.when(r == 0)
    def _():
        cw = cw_ref[...]
        cb = cb_ref[...]
        x0 = _short_conv(x0_ref[0].astype(F32), cw[:, 0:C], cb[:, 0:C])
        x1 = _short_conv(x1_ref[0].astype(F32), cw[:, C:2 * C], cb[:, C:2 * C])
        v = _short_conv(v_ref[0].astype(F32), cw[:, 2 * C:], cb[:, 2 * C:])
        z = v * x1
        z_ref[...] = z.astype(BF16)
        p1_ref[0] = x0.astype(BF16)
        p2_ref[0] = (x0 * z * bias_ref[...]).astype(BF16)

    z = z_ref[...]
    a = _dot(wre_ref[...], z)
    q = _dot(wim_ref[...], z)
    ka = kre_ref[...]
    kq = kim_ref[...]
    first = jnp.logical_and(r == 0, lax.broadcasted_iota(jnp.int32, a.shape, 0) == 0)
    yr = jnp.where(first, a * ka, a * ka - q * kq)
    yq = jnp.where(first, q * kq, a * kq + q * ka)
    sc = jnp.where(first, 1.0 / FFT_N, 2.0 / FFT_N)
    y_ref[0, 0] = (yr * sc).astype(BF16)
    y_ref[0, 1] = (yq * sc).astype(BF16)


HY_TR = 512


def _hyfwd(proj3, cw, cb, bias, wf, kp):
    B = proj3.shape[0]
    nr = SEQ // HY_TR
    cblk = lambda c: pl.BlockSpec((1, SEQ, HYENA_WIDTH), lambda b, r: (b, 0, c))
    full = lambda a: pl.BlockSpec(a.shape, lambda b, r: (0, 0))
    seq_out = pl.BlockSpec((1, SEQ, HYENA_WIDTH), lambda b, r: (b, 0, 0))
    y, p1, p2 = pl.pallas_call(
        _hyfwd_kernel,
        grid=(B, nr),
        in_specs=[cblk(COL_HY // HYENA_WIDTH), cblk(COL_HY // HYENA_WIDTH + 1), cblk(COL_HY // HYENA_WIDTH + 2),
                  full(cw), full(cb), full(bias),
                  pl.BlockSpec((HY_TR, SEQ), lambda b, r: (r, 0)),
                  pl.BlockSpec((HY_TR, SEQ), lambda b, r: (r + nr, 0)),
                  pl.BlockSpec((HY_TR, HYENA_WIDTH), lambda b, r: (r, 0)),
                  pl.BlockSpec((HY_TR, HYENA_WIDTH), lambda b, r: (r + nr, 0))],
        out_specs=[pl.BlockSpec((1, 2, HY_TR, HYENA_WIDTH), lambda b, r: (b, 0, r, 0)), seq_out, seq_out],
        out_shape=[jax.ShapeDtypeStruct((B, 2, SEQ, HYENA_WIDTH), BF16),
                   jax.ShapeDtypeStruct((B, SEQ, HYENA_WIDTH), BF16),
                   jax.ShapeDtypeStruct((B, SEQ, HYENA_WIDTH), BF16)],
        scratch_shapes=[pltpu.VMEM((SEQ, HYENA_WIDTH), BF16)],
        compiler_params=_cparams(("parallel", "arbitrary")),
        name="hyfwd",
    )(proj3, proj3, proj3, cw, cb, bias, wf, wf, kp, kp)
    return y.reshape(B, FFT_N, HYENA_WIDTH), p1, p2


HYI_TT = 512


def _hyinv_kernel(wt_ref, y_ref, p1_ref, p2_ref, o_ref):
    conv = _dot(wt_ref[...], y_ref[0])
    o_ref[0] = (p1_ref[0].astype(F32) * conv + p2_ref[0].astype(F32)).astype(BF16)


def _hyinv(wft, ys, p1, p2):
    B = ys.shape[0]
    blk = pl.BlockSpec((1, HYI_TT, HYENA_WIDTH), lambda t, b: (b, t, 0))
    return pl.pallas_call(
        _hyinv_kernel,
        grid=(SEQ // HYI_TT, B),
        in_specs=[pl.BlockSpec((HYI_TT, FFT_N), lambda t, b: (t, 0)),
                  pl.BlockSpec((1, FFT_N, HYENA_WIDTH), lambda t, b: (b, 0, 0)),
                  blk, blk],
        out_specs=blk,
        out_shape=jax.ShapeDtypeStruct((B, SEQ, HYENA_WIDTH), BF16),
        compiler_params=_cparams(("parallel", "parallel")),
        name="hyinv",
    )(wft, ys, p1, p2)


MERGE_TM = 512
QP_SLOTS = PEER_HEADS * 2


def _merge_kernel(att_ref, hy_ref, gl_ref, bg_ref, x_ref, woa_ref, woh_ref, wout_ref, fg_ref, wq_ref,
                  h_ref, hn_ref, qp_ref):
    a = _dot(att_ref[...], woa_ref[...])
    yh = _dot(hy_ref[...], woh_ref[...])
    g = 1.0 / (1.0 + jnp.exp(-(gl_ref[...].astype(F32) + bg_ref[...])))
    merged = (g[:, :D_MODEL] * a + g[:, D_MODEL:] * yh).astype(BF16)
    h = x_ref[...] + _dot(merged, wout_ref[...])
    h_ref[...] = h
    hn = _rms(h, fg_ref[...])
    hn_ref[...] = (hn * INV_SQRT2).astype(BF16)
    qp = _dot(hn.astype(BF16), wq_ref[...])
    for c in range(QP_SLOTS):
        qp_ref[c] = qp[:, c * LANES:(c + 1) * LANES].astype(BF16)


def _merge(att, hy, proj, bg, x2, woa, woh, wout, fg, wq):
    T = x2.shape[0]
    half = PEER_KEY_DIM // 2
    row = lambda w: pl.BlockSpec((MERGE_TM, w), lambda i: (i, 0))
    full = lambda a: pl.BlockSpec(a.shape, lambda i: (0, 0))
    return pl.pallas_call(
        _merge_kernel,
        grid=(T // MERGE_TM,),
        in_specs=[row(att.shape[1]), row(hy.shape[1]),
                  pl.BlockSpec((MERGE_TM, 2 * D_MODEL), lambda i: (i, COL_GATE // (2 * D_MODEL))),
                  full(bg), row(D_MODEL), full(woa), full(woh), full(wout), full(fg), full(wq)],
        out_specs=[row(D_MODEL), row(D_MODEL),
                   pl.BlockSpec((QP_SLOTS, MERGE_TM, half), lambda i: (0, i, 0))],
        out_shape=[jax.ShapeDtypeStruct((T, D_MODEL), F32),
                   jax.ShapeDtypeStruct((T, D_MODEL), BF16),
                   jax.ShapeDtypeStruct((QP_SLOTS, T, half), BF16)],
        compiler_params=_cparams(("parallel",)),
        name="merge",
    )(att, hy, proj, bg, x2, woa, woh, wout, fg, wq)


ROUTE_TB = LANES
NEG_INF = float("-inf")
N_CHAINS = 2 * PEER_HEADS
CAND_ROWS = 72


def _extract_top(work_ref, vals_ref, n_chains):
    slot = lax.broadcasted_iota(jnp.int32, vals_ref.shape[1:], 0)

    def body(k, carry):
        for c in range(n_chains):
            s = work_ref[c]
            m = jnp.max(s, axis=0, keepdims=True)
            work_ref[c] = jnp.where(s == m, NEG_INF, s)
            vals_ref[c] = jnp.where(slot == k, m, vals_ref[c])
        return carry
    lax.fori_loop(0, PEER_TOPK, body, 0)


def _candidates(v1, v2):
    a16 = lax.broadcasted_iota(jnp.int32, v1.shape, 0)
    a8 = a16[0:8]
    return jnp.concatenate([
        v1[0:1] + v2,
        v1[1:2] + v2[0:8], v1[2:3] + v2[0:8], v1[3:4] + v2[0:8],
        jnp.where(a16 >= 4, v1 + v2[0:1], NEG_INF),
        jnp.where(a8 >= 4, v1[0:8] + v2[1:2], NEG_INF),
        jnp.where(a8 == 4, v1[0:8] + v2[2:3], NEG_INF),
    ], axis=0)


def _route_kernel(q_ref, k1_ref, k2_ref, cnt_ref, a_ref, rank_ref, b_ref,
                  s_ref, work_ref, vals_ref, cand_ref, cvals_ref):
    for h in range(PEER_HEADS):
        for side, k_ref in enumerate((k1_ref, k2_ref)):
            s = _dot_nt(k_ref[h], q_ref[2 * h + side])
            s_ref[2 * h + side] = s
            work_ref[2 * h + side] = s
    vals_ref[...] = jnp.zeros_like(vals_ref)
    _extract_top(work_ref, vals_ref, N_CHAINS)
    for h in range(PEER_HEADS):
        cand_ref[h] = _candidates(vals_ref[2 * h], vals_ref[2 * h + 1])
    cvals_ref[...] = jnp.zeros_like(cvals_ref)
    _extract_top(cand_ref, cvals_ref, PEER_HEADS)

    for h in range(PEER_HEADS):
        v1, v2, cv = vals_ref[2 * h], vals_ref[2 * h + 1], cvals_ref[h]
        s1, s2 = s_ref[2 * h], s_ref[2 * h + 1]
        tau = cv[PEER_TOPK - 1:PEER_TOPK]
        zsum = jnp.sum(jnp.exp(cv - cv[0:1]), axis=0, keepdims=True)
        cnt_sorted = jnp.zeros_like(v1)
        for b in range(PEER_TOPK):
            cnt_sorted = cnt_sorted + jnp.where(v1 + v2[b:b + 1] >= tau, 1.0, 0.0)
        cnt = jnp.zeros_like(s1)
        for a in range(PEER_TOPK):
            cnt = jnp.where(s1 == v1[a:a + 1], cnt_sorted[a:a + 1], cnt)
        rank = jnp.full_like(s2, float(PEER_TOPK))
        for b in reversed(range(PEER_TOPK)):
            rank = jnp.where(s2 >= v2[b:b + 1], float(b), rank)
        cnt_ref[0, h] = cnt * RANK_SCALE
        a_ref[0, h] = jnp.exp(s1 - v1[0:1]) * (GELU_SCALE / zsum)
        rank_ref[0, h] = (rank * RANK_SCALE).astype(BF16)
        b_ref[0, h] = jnp.exp(s2 - v2[0:1]).astype(BF16)


def _route(qp, k1, k2):
    T = qp.shape[1]
    half = PEER_KEY_DIM // 2
    blk = pl.BlockSpec((1, PEER_HEADS, N_KEYS, LANES), lambda i: (i, 0, 0, 0))
    shape = lambda dt: jax.ShapeDtypeStruct((T // LANES, PEER_HEADS, N_KEYS, LANES), dt)
    keys = pl.BlockSpec((PEER_HEADS, N_KEYS, half), lambda i: (0, 0, 0))
    return pl.pallas_call(
        _route_kernel,
        grid=(T // ROUTE_TB,),
        in_specs=[pl.BlockSpec((QP_SLOTS, ROUTE_TB, half), lambda i: (0, i, 0)), keys, keys],
        out_specs=[blk, blk, blk, blk],
        out_shape=[shape(F32), shape(F32), shape(BF16), shape(BF16)],
        scratch_shapes=[pltpu.VMEM((N_CHAINS, N_KEYS, ROUTE_TB), F32),
                        pltpu.VMEM((N_CHAINS, N_KEYS, ROUTE_TB), F32),
                        pltpu.VMEM((N_CHAINS, PEER_TOPK, ROUTE_TB), F32),
                        pltpu.VMEM((PEER_HEADS, CAND_ROWS, ROUTE_TB), F32),
                        pltpu.VMEM((PEER_HEADS, PEER_TOPK, ROUTE_TB), F32)],
        compiler_params=_cparams(("parallel",)),
        name="route",
    )(qp, k1, k2)


PEER_TB = 512
PEER_TE = 1024
PEER_ROWS = PEER_TE // N_KEYS
PEER_NC = PEER_TB // LANES
INV_SQRT2 = 1.0 / math.sqrt(2.0)
GELU_SCALE = 0.5 * math.sqrt(2.0)
RANK_SCALE = 256.0


def _row_tile(ref, c, h, ii):
    return jnp.broadcast_to(ref[c, h, ii:ii + 1, :], (N_KEYS, LANES)).astype(BF16)


def _peer_kernel(hn_ref, u_ref, vt_ref, cnt_ref, a_ref, rank_ref, b_ref, y_ref, act_ref, wt_ref):
    e = pl.program_id(1)

    @pl.when(e == 0)
    def _():
        y_ref[...] = jnp.zeros_like(y_ref)

    act = _dot_nt(u_ref[...], hn_ref[...])
    for c in range(PEER_NC):
        act_ref[c] = act[:, c * LANES:(c + 1) * LANES]

    def chunk(c, carry):
        for ii in range(PEER_ROWS):
            rs = slice(ii * N_KEYS, (ii + 1) * N_KEYS)
            w = jnp.zeros((N_KEYS, LANES), BF16)
            for h in range(PEER_HEADS):
                cnt = _row_tile(cnt_ref, c, h, ii)
                wa = _row_tile(a_ref, c, h, ii)
                sel = jnp.minimum(jnp.maximum(cnt - rank_ref[c, h], 0), b_ref[c, h])
                w = w + sel * wa
            x = act_ref[c, rs, :]
            wt_ref[c, rs, :] = (x * (1.0 + lax.erf(x))).astype(BF16) * w
        return carry
    lax.fori_loop(0, PEER_NC, chunk, 0)

    wt = jnp.concatenate([wt_ref[c] for c in range(PEER_NC)], axis=1)
    y_ref[...] += _dot(vt_ref[...], wt)


def _peer(hn, u, vt, cnt, a, rank, b):
    T = hn.shape[0]
    rows = pl.BlockSpec((PEER_NC, PEER_HEADS, PEER_ROWS, LANES), lambda t, e: (t, 0, e, 0))
    keys = pl.BlockSpec((PEER_NC, PEER_HEADS, N_KEYS, LANES), lambda t, e: (t, 0, 0, 0))
    return pl.pallas_call(
        _peer_kernel,
        grid=(T // PEER_TB, N_EXPERTS // PEER_TE),
        in_specs=[pl.BlockSpec((PEER_TB, D_MODEL), lambda t, e: (t, 0)),
                  pl.BlockSpec((PEER_TE, D_MODEL), lambda t, e: (e, 0)),
                  pl.BlockSpec((D_MODEL, PEER_TE), lambda t, e: (0, e)),
                  rows, rows, keys, keys],
        out_specs=pl.BlockSpec((D_MODEL, PEER_TB), lambda t, e: (0, t)),
        out_shape=jax.ShapeDtypeStruct((D_MODEL, T), F32),
        scratch_shapes=[pltpu.VMEM((PEER_NC, PEER_TE, LANES), F32),
                        pltpu.VMEM((PEER_NC, PEER_TE, LANES), BF16)],
        compiler_params=_cparams(("parallel", "arbitrary")),
        name="peer",
    )(hn, u, vt, cnt, a, rank, b)


def _regroup_w_in(w):
    o_kv = Q_LORA_RANK
    o_pe = o_kv + KV_LORA_RANK
    o_hy = o_pe + QK_ROPE_DIM
    o_gate = o_hy + 3 * HYENA_WIDTH
    out = jnp.zeros((D_MODEL, PROJ_COLS), BF16)
    out = out.at[:, COL_CQ:COL_CQ + Q_LORA_RANK].set(w[:, :o_kv].astype(BF16))
    out = out.at[:, COL_CKV:COL_CKV + KV_LORA_RANK].set(w[:, o_kv:o_pe].astype(BF16))
    out = out.at[:, COL_KPE + KPE_LANE:COL_KPE + KPE_LANE + QK_ROPE_DIM].set(w[:, o_pe:o_hy].astype(BF16))
    out = out.at[:, COL_HY:COL_HY + 3 * HYENA_WIDTH].set(w[:, o_hy:o_gate].astype(BF16))
    out = out.at[:, COL_GATE:].set(w[:, o_gate:].astype(BF16))
    return out


def _head_slots(w, width):
    k = w.shape[0]
    w3 = w.reshape(k, MLA_HEADS, width).astype(BF16)
    return jnp.zeros((k, MLA_HEADS, HEAD_SLOT), BF16).at[:, :, :width].set(w3).reshape(k, HW)


def kernel(x, attn_norm, w_in, b_gate, q_a_norm, w_uq, kv_a_norm, w_ukv, q_norm, k_norm, w_o_attn, hyena_conv_w, hyena_conv_b, filt_w1, filt_b1, filt_w2, filt_b2, filt_w3, filt_b3, filt_w4, filt_b4, filt_freq, hyena_bias, w_o_hyena, w_out, ffn_norm, peer_w_q, peer_keys1, peer_keys2, expert_u, expert_v):
    B = x.shape[0]
    T = B * SEQ
    x2 = x.reshape(T, D_MODEL)
    bf = lambda a: a.astype(BF16)

    proj = _inproj(x2, attn_norm, _regroup_w_in(w_in[0]))
    kv3 = w_ukv[0].reshape(KV_LORA_RANK, MLA_HEADS, QK_NOPE_DIM + V_HEAD_DIM)
    wuk_p = _head_slots(kv3[:, :, :QK_NOPE_DIM].reshape(KV_LORA_RANK, -1), QK_NOPE_DIM)
    wuv_p = _head_slots(kv3[:, :, QK_NOPE_DIM:].reshape(KV_LORA_RANK, -1), V_HEAD_DIM)
    vone = jnp.zeros((MLA_HEADS, HEAD_SLOT), F32).at[:, V_HEAD_DIM].set(1.0).reshape(1, HW)
    q, k, v = _mlaprep(proj, q_a_norm, kv_a_norm, _head_slots(w_uq[0], QK_HEAD_DIM), wuk_p, wuv_p,
                       _pad2(q_norm, 1, HEAD_SLOT), _pad2(k_norm, 1, HEAD_SLOT), vone)
    att = _attention(q, k, v, B).reshape(T, MLA_HEADS * V_HEAD_DIM)

    wf, wft = (jnp.asarray(w).astype(BF16) for w in _dft_tables())
    hsd = _filter(filt_w1[0], filt_b1[0], filt_w2[0], filt_b2[0], filt_w3[0], filt_b3[0], filt_w4[0],
                  filt_b4[0], filt_freq[0])
    kp = _kspec(wf, hsd)
    ys, p1, p2 = _hyfwd(proj.reshape(B, SEQ, PROJ_COLS), hyena_conv_w[0], hyena_conv_b, hyena_bias, wf, kp)
    hy = _hyinv(wft, ys, p1, p2).reshape(T, HYENA_WIDTH)

    h, hn, qp = _merge(att, hy, proj, b_gate, x2, bf(w_o_attn[0]), bf(w_o_hyena[0]), bf(w_out[0]),
                       ffn_norm, bf(peer_w_q[0]))

    cnt, a, rank, b = _route(qp, bf(peer_keys1[0]), bf(peer_keys2[0]))
    yt = _peer(hn, bf(expert_u[0]), bf(expert_v[0]).T, cnt, a, rank, b)
    return (h + yt.T).reshape(B, SEQ, D_MODEL)
```

```python
import functools
import math

import numpy as np
import jax
import jax.numpy as jnp
from jax import lax
from jax.experimental import pallas as pl
from jax.experimental.pallas import tpu as pltpu

F32 = jnp.float32
BF16 = jnp.bfloat16

D_MODEL = 1024
SEQ = 2048
MLA_HEADS = 8
QK_NOPE_DIM = 64
QK_ROPE_DIM = 32
QK_HEAD_DIM = QK_NOPE_DIM + QK_ROPE_DIM
V_HEAD_DIM = 64
Q_LORA_RANK = 256
KV_LORA_RANK = 128
ROPE_THETA = 10000.0
HYENA_WIDTH = 512
FILTER_EMB = 33
FILTER_ORDER = 64
FAST_DECAY_PCT = 0.3
SLOW_DECAY_PCT = 1.5
DECAY_TARGET = 1e-2
PEER_HEADS = 8
N_KEYS = 128
N_EXPERTS = N_KEYS * N_KEYS
PEER_KEY_DIM = 256
PEER_TOPK = 16
EPS = 1e-6

LANES = 128
HEAD_SLOT = LANES
FFT_N = 2 * SEQ
PROJ_COLS = 4096
COL_CQ, COL_CKV, COL_KPE, COL_HY, COL_GATE = 0, 256, 384, 512, 2048
KPE_LANE = QK_NOPE_DIM
VMEM_LIMIT = 56 * 1024 * 1024


def _cparams(sem):
    return pltpu.CompilerParams(dimension_semantics=sem, vmem_limit_bytes=VMEM_LIMIT)


def _dot(a, b):
    return jnp.dot(a, b, preferred_element_type=F32)


def _dot_nt(a, b):
    return lax.dot_general(a, b, (((1,), (1,)), ((), ())), preferred_element_type=F32)


def _rms(x, g):
    return x * lax.rsqrt(jnp.mean(x * x, axis=-1, keepdims=True) + EPS) * g


@functools.lru_cache(maxsize=None)
def _dft_tables():
    p = np.arange(FFT_N, dtype=np.int64)[:, None]
    s = np.arange(SEQ, dtype=np.int64)[None, :]
    f = np.where(p <= SEQ, p, p - SEQ)
    ang = (2.0 * np.pi / FFT_N) * ((f * s) % FFT_N).astype(np.float64)
    w = np.where(p <= SEQ, np.cos(ang), np.sin(ang)).astype(np.float32)
    return w, np.ascontiguousarray(w.T)


@functools.lru_cache(maxsize=None)
def _rotary_tables():
    half = QK_ROPE_DIM // 2
    inv_freq = ROPE_THETA ** (-np.arange(half, dtype=np.float64) / half)
    ang = np.arange(SEQ, dtype=np.float64)[:, None] * inv_freq[None, :]
    cos = np.ones((SEQ, HEAD_SLOT)); sa = np.zeros((SEQ, HEAD_SLOT)); sb = np.zeros((SEQ, HEAD_SLOT))
    lo, mid, hi = QK_NOPE_DIM, QK_NOPE_DIM + half, QK_HEAD_DIM
    cos[:, lo:mid] = np.cos(ang); cos[:, mid:hi] = np.cos(ang)
    sb[:, lo:mid] = -np.sin(ang)
    sa[:, mid:hi] = np.sin(ang)
    return tuple(jnp.asarray(t, dtype=F32) for t in (cos, sa, sb))


@functools.lru_cache(maxsize=None)
def _filter_consts():
    L = SEQ
    t = np.linspace(0.0, 1.0, L)[:, None]
    bands = (FILTER_EMB - 1) // 2
    w = 2.0 * math.pi * np.arange(L)[:, None] / L
    f = np.linspace(1e-4, bands - 1, bands)[None, :]
    z = np.concatenate([t, np.cos(f * w), -np.sin(f * w)], axis=-1)
    zp = np.zeros((L, LANES)); zp[:, :FILTER_EMB] = z
    min_decay = math.log(DECAY_TARGET) / SLOW_DECAY_PCT
    max_decay = math.log(DECAY_TARGET) / FAST_DECAY_PCT
    deltas = np.abs(np.linspace(min_decay, max_decay, HYENA_WIDTH))[None, :]
    return jnp.asarray(zp, dtype=F32), jnp.asarray(deltas, dtype=F32)


INPROJ_TM = 512
INPROJ_CW = 1024


def _inproj_kernel(x_ref, g_ref, w_ref, o_ref):
    xn = _rms(x_ref[...], g_ref[...]).astype(BF16)
    for c in range(PROJ_COLS // INPROJ_CW):
        sl = slice(c * INPROJ_CW, (c + 1) * INPROJ_CW)
        o_ref[:, sl] = _dot(xn, w_ref[:, sl]).astype(BF16)


def _inproj(x2, g, w_p):
    T = x2.shape[0]
    return pl.pallas_call(
        _inproj_kernel,
        grid=(T // INPROJ_TM,),
        in_specs=[pl.BlockSpec((INPROJ_TM, D_MODEL), lambda i: (i, 0)),
                  pl.BlockSpec((1, D_MODEL), lambda i: (0, 0)),
                  pl.BlockSpec((D_MODEL, PROJ_COLS), lambda i: (0, 0))],
        out_specs=pl.BlockSpec((INPROJ_TM, PROJ_COLS), lambda i: (i, 0)),
        out_shape=jax.ShapeDtypeStruct((T, PROJ_COLS), BF16),
        compiler_params=_cparams(("parallel",)),
        name="inproj",
    )(x2, g, w_p)


MLA_TM = 512
HW = MLA_HEADS * HEAD_SLOT


def _rope(x, cos, sa, sb):
    half = QK_ROPE_DIM // 2
    return x * cos + pltpu.roll(x, half, 1) * sa + pltpu.roll(x, HEAD_SLOT - half, 1) * sb


def _mlaprep_kernel(p_ref, qa_ref, kva_ref, wuq_ref, wuk_ref, wuv_ref, gq_ref, gk_ref, vone_ref,
                    cos_ref, sa_ref, sb_ref, q_ref, k_ref, v_ref):
    p = p_ref[...].astype(F32)
    cqn = _rms(p[:, COL_CQ:COL_CQ + Q_LORA_RANK], qa_ref[...]).astype(BF16)
    ckvn = _rms(p[:, COL_CKV:COL_CKV + KV_LORA_RANK], kva_ref[...]).astype(BF16)
    kpe = p[:, COL_KPE:COL_KPE + HEAD_SLOT]
    q = _dot(cqn, wuq_ref[...])
    kn = _dot(ckvn, wuk_ref[...])
    v_ref[...] = (_dot(ckvn, wuv_ref[...]) + vone_ref[...]).astype(BF16)
    cos, sa, sb = cos_ref[...], sa_ref[...], sb_ref[...]
    scale = QK_HEAD_DIM ** -0.5 * math.log2(math.e)
    inv_d = 1.0 / QK_HEAD_DIM
    for h in range(MLA_HEADS):
        sl = slice(h * HEAD_SLOT, (h + 1) * HEAD_SLOT)
        qh = q[:, sl]
        qh = qh * lax.rsqrt(jnp.sum(qh * qh, axis=-1, keepdims=True) * inv_d + EPS) * gq_ref[...]
        q_ref[:, sl] = (_rope(qh, cos, sa, sb) * scale).astype(BF16)
        kh = kn[:, sl] + kpe
        kh = kh * lax.rsqrt(jnp.sum(kh * kh, axis=-1, keepdims=True) * inv_d + EPS) * gk_ref[...]
        k_ref[:, sl] = _rope(kh, cos, sa, sb).astype(BF16)


def _mlaprep(proj, qa, kva, wuq_p, wuk_p, wuv_p, gq, gk, vone):
    T = proj.shape[0]
    cos, sa, sb = _rotary_tables()
    nseq = SEQ // MLA_TM
    full = lambda shape: pl.BlockSpec(shape, lambda i: (0, 0))
    pos = pl.BlockSpec((MLA_TM, HEAD_SLOT), lambda i: (i % nseq, 0))
    out = pl.BlockSpec((MLA_TM, HW), lambda i: (i, 0))
    return pl.pallas_call(
        _mlaprep_kernel,
        grid=(T // MLA_TM,),
        in_specs=[pl.BlockSpec((MLA_TM, 512), lambda i: (i, 0)),
                  full((1, Q_LORA_RANK)), full((1, KV_LORA_RANK)),
                  full((Q_LORA_RANK, HW)), full((KV_LORA_RANK, HW)), full((KV_LORA_RANK, HW)),
                  full((1, HEAD_SLOT)), full((1, HEAD_SLOT)), full((1, HW)),
                  pos, pos, pos],
        out_specs=[out, out, out],
        out_shape=[jax.ShapeDtypeStruct((T, HW), BF16)] * 3,
        compiler_params=_cparams(("parallel",)),
        name="mlaprep",
    )(proj, qa, kva, wuq_p, wuk_p, wuv_p, gq, gk, vone, cos, sa, sb)


ATT_TQ = 512


def _attn_kernel(q_ref, k_ref, v_ref, o_ref):
    outs = []
    for hh in range(2):
        sl = slice(hh * HEAD_SLOT, (hh + 1) * HEAD_SLOT)
        s = _dot_nt(q_ref[0, :, sl], k_ref[0, :, sl])
        m = jnp.max(s, axis=-1, keepdims=True)
        p = jnp.exp2(s - m).astype(BF16)
        pv = _dot(p, v_ref[0, :, sl])
        outs.append(pv[:, :V_HEAD_DIM] / pv[:, V_HEAD_DIM:V_HEAD_DIM + 1])
    o_ref[0] = jnp.concatenate(outs, axis=-1).astype(BF16)


def _attention(q, k, v, B):
    q3, k3, v3 = (a.reshape(B, SEQ, HW) for a in (q, k, v))
    pair = 2 * HEAD_SLOT
    return pl.pallas_call(
        _attn_kernel,
        grid=(B, MLA_HEADS // 2, SEQ // ATT_TQ),
        in_specs=[pl.BlockSpec((1, ATT_TQ, pair), lambda b, h, i: (b, i, h)),
                  pl.BlockSpec((1, SEQ, pair), lambda b, h, i: (b, 0, h)),
                  pl.BlockSpec((1, SEQ, pair), lambda b, h, i: (b, 0, h))],
        out_specs=pl.BlockSpec((1, ATT_TQ, 2 * V_HEAD_DIM), lambda b, h, i: (b, i, h)),
        out_shape=jax.ShapeDtypeStruct((B, SEQ, MLA_HEADS * V_HEAD_DIM), BF16),
        compiler_params=_cparams(("parallel", "parallel", "parallel")),
        name="attn",
    )(q3, k3, v3)


FILT_TL = 512


def _filter_kernel(z_ref, w1_ref, b1_ref, w2_ref, b2_ref, w3_ref, b3_ref, w4_ref, b4_ref, fr_ref,
                   dl_ref, o_ref):
    hp = functools.partial(jnp.dot, precision=lax.Precision.HIGHEST, preferred_element_type=F32)
    z = z_ref[...]
    fr = fr_ref[...]
    h = jnp.sin(fr * (hp(z, w1_ref[...]) + b1_ref[...]))
    h = jnp.sin(fr * (hp(h, w2_ref[...]) + b2_ref[...]))
    h = jnp.sin(fr * (hp(h, w3_ref[...]) + b3_ref[...]))
    h4 = hp(h, w4_ref[...]) + b4_ref[...]
    decay = jnp.exp(-z[:, 0:1] * dl_ref[...])
    hf = h4[:, :HYENA_WIDTH] * decay
    hb = h4[:, HYENA_WIDTH:] * decay
    row = pl.program_id(0) * FILT_TL + lax.broadcasted_iota(jnp.int32, hb.shape, 0)
    hb = jnp.where(row == 0, 0.0, hb)
    o_ref[:, :HYENA_WIDTH] = (hf + hb).astype(BF16)
    o_ref[:, HYENA_WIDTH:] = (hf - hb).astype(BF16)


def _pad2(a, r, c):
    return jnp.zeros((r, c), a.dtype).at[:a.shape[0], :a.shape[1]].set(a)


def _filter(w1, b1, w2, b2, w3, b3, w4, b4, freq):
    z, deltas = _filter_consts()
    P = LANES
    args = (z, _pad2(w1, P, P), _pad2(b1[None], 1, P), _pad2(w2, P, P), _pad2(b2[None], 1, P),
            _pad2(w3, P, P), _pad2(b3[None], 1, P), _pad2(w4, P, 2 * HYENA_WIDTH), b4[None],
            _pad2(freq[None], 1, P), deltas)
    full = lambda a: pl.BlockSpec(a.shape, lambda i: (0, 0))
    return pl.pallas_call(
        _filter_kernel,
        grid=(SEQ // FILT_TL,),
        in_specs=[pl.BlockSpec((FILT_TL, P), lambda i: (i, 0))] + [full(a) for a in args[1:]],
        out_specs=pl.BlockSpec((FILT_TL, 2 * HYENA_WIDTH), lambda i: (i, 0)),
        out_shape=jax.ShapeDtypeStruct((SEQ, 2 * HYENA_WIDTH), BF16),
        compiler_params=_cparams(("parallel",)),
        name="filt",
    )(*args)


SPEC_TR = 512


def _kspec_kernel(w_ref, h_ref, o_ref):
    r = _dot(w_ref[...], h_ref[...])
    row = pl.program_id(0) * SPEC_TR + lax.broadcasted_iota(jnp.int32, (SPEC_TR, HYENA_WIDTH), 0)
    o_ref[...] = jnp.where(row <= SEQ, r[:, :HYENA_WIDTH], r[:, HYENA_WIDTH:])


def _kspec(wf, hsd):
    return pl.pallas_call(
        _kspec_kernel,
        grid=(FFT_N // SPEC_TR,),
        in_specs=[pl.BlockSpec((SPEC_TR, SEQ), lambda i: (i, 0)),
                  pl.BlockSpec((SEQ, 2 * HYENA_WIDTH), lambda i: (0, 0))],
        out_specs=pl.BlockSpec((SPEC_TR, HYENA_WIDTH), lambda i: (i, 0)),
        out_shape=jax.ShapeDtypeStruct((FFT_N, HYENA_WIDTH), F32),
        compiler_params=_cparams(("parallel",)),
        name="kspec",
    )(wf, hsd)


def _short_conv(u, w, b):
    row = lax.broadcasted_iota(jnp.int32, u.shape, 0)
    prev = jnp.where(row == 0, 0.0, pltpu.roll(u, 1, 0))
    nxt = jnp.where(row == SEQ - 1, 0.0, pltpu.roll(u, SEQ - 1, 0))
    return prev * w[0:1] + u * w[1:2] + nxt * w[2:3] + b


def _hyfwd_kernel(x0_ref, x1_ref, v_ref, cw_ref, cb_ref, bias_ref, wre_ref, wim_ref, kre_ref, kim_ref,
                  y_ref, p1_ref, p2_ref, z_ref):
    r = pl.program_id(1)
    C = HYENA_WIDTH

    @pl.when(r == 0)
    def _():
        cw = cw_ref[...]
        cb = cb_ref[...]
        x0 = _short_conv(x0_ref[0].astype(F32), cw[:, 0:C], cb[:, 0:C])
        x1 = _short_conv(x1_ref[0].astype(F32), cw[:, C:2 * C], cb[:, C:2 * C])
        v = _short_conv(v_ref[0].astype(F32), cw[:, 2 * C:], cb[:, 2 * C:])
        z = v * x1
        z_ref[...] = z.astype(BF16)
        p1_ref[0] = x0.astype(BF16)
        p2_ref[0] = (x0 * z * bias_ref[...]).astype(BF16)

    z = z_ref[...]
    a = _dot(wre_ref[...], z)
    q = _dot(wim_ref[...], z)
    ka = kre_ref[...]
    kq = kim_ref[...]
    first = jnp.logical_and(r == 0, lax.broadcasted_iota(jnp.int32, a.shape, 0) == 0)
    yr = jnp.where(first, a * ka, a * ka - q * kq)
    yq = jnp.where(first, q * kq, a * kq + q * ka)
    sc = jnp.where(first, 1.0 / FFT_N, 2.0 / FFT_N)
    y_ref[0, 0] = (yr * sc).astype(BF16)
    y_ref[0, 1] = (yq * sc).astype(BF16)


HY_TR = 512


def _hyfwd(proj3, cw, cb, bias, wf, kp):
    B = proj3.shape[0]
    nr = SEQ // HY_TR
    cblk = lambda c: pl.BlockSpec((1, SEQ, HYENA_WIDTH), lambda b, r: (b, 0, c))
    full = lambda a: pl.BlockSpec(a.shape, lambda b, r: (0, 0))
    seq_out = pl.BlockSpec((1, SEQ, HYENA_WIDTH), lambda b, r: (b, 0, 0))
    y, p1, p2 = pl.pallas_call(
        _hyfwd_kernel,
        grid=(B, nr),
        in_specs=[cblk(COL_HY // HYENA_WIDTH), cblk(COL_HY // HYENA_WIDTH + 1), cblk(COL_HY // HYENA_WIDTH + 2),
                  full(cw), full(cb), full(bias),
                  pl.BlockSpec((HY_TR, SEQ), lambda b, r: (r, 0)),
                  pl.BlockSpec((HY_TR, SEQ), lambda b, r: (r + nr, 0)),
                  pl.BlockSpec((HY_TR, HYENA_WIDTH), lambda b, r: (r, 0)),
                  pl.BlockSpec((HY_TR, HYENA_WIDTH), lambda b, r: (r + nr, 0))],
        out_specs=[pl.BlockSpec((1, 2, HY_TR, HYENA_WIDTH), lambda b, r: (b, 0, r, 0)), seq_out, seq_out],
        out_shape=[jax.ShapeDtypeStruct((B, 2, SEQ, HYENA_WIDTH), BF16),
                   jax.ShapeDtypeStruct((B, SEQ, HYENA_WIDTH), BF16),
                   jax.ShapeDtypeStruct((B, SEQ, HYENA_WIDTH), BF16)],
        scratch_shapes=[pltpu.VMEM((SEQ, HYENA_WIDTH), BF16)],
        compiler_params=_cparams(("parallel", "arbitrary")),
        name="hyfwd",
    )(proj3, proj3, proj3, cw, cb, bias, wf, wf, kp, kp)
    return y.reshape(B, FFT_N, HYENA_WIDTH), p1, p2


HYI_TT = 512


def _hyinv_kernel(wt_ref, y_ref, p1_ref, p2_ref, o_ref):
    conv = _dot(wt_ref[...], y_ref[0])
    o_ref[0] = (p1_ref[0].astype(F32) * conv + p2_ref[0].astype(F32)).astype(BF16)


def _hyinv(wft, ys, p1, p2):
    B = ys.shape[0]
    blk = pl.BlockSpec((1, HYI_TT, HYENA_WIDTH), lambda t, b: (b, t, 0))
    return pl.pallas_call(
        _hyinv_kernel,
        grid=(SEQ // HYI_TT, B),
        in_specs=[pl.BlockSpec((HYI_TT, FFT_N), lambda t, b: (t, 0)),
                  pl.BlockSpec((1, FFT_N, HYENA_WIDTH), lambda t, b: (b, 0, 0)),
                  blk, blk],
        out_specs=blk,
        out_shape=jax.ShapeDtypeStruct((B, SEQ, HYENA_WIDTH), BF16),
        compiler_params=_cparams(("parallel", "parallel")),
        name="hyinv",
    )(wft, ys, p1, p2)


MERGE_TM = 512
QP_SLOTS = PEER_HEADS * 2


def _merge_kernel(att_ref, hy_ref, gl_ref, bg_ref, x_ref, woa_ref, woh_ref, wout_ref, fg_ref, wq_ref,
                  h_ref, hn_ref, qp_ref):
    a = _dot(att_ref[...], woa_ref[...])
    yh = _dot(hy_ref[...], woh_ref[...])
    g = 1.0 / (1.0 + jnp.exp(-(gl_ref[...].astype(F32) + bg_ref[...])))
    merged = (g[:, :D_MODEL] * a + g[:, D_MODEL:] * yh).astype(BF16)
    h = x_ref[...] + _dot(merged, wout_ref[...])
    h_ref[...] = h
    hn = _rms(h, fg_ref[...])
    hn_ref[...] = (hn * INV_SQRT2).astype(BF16)
    qp = _dot(hn.astype(BF16), wq_ref[...])
    for c in range(QP_SLOTS):
        qp_ref[c] = qp[:, c * LANES:(c + 1) * LANES].astype(BF16)


def _merge(att, hy, proj, bg, x2, woa, woh, wout, fg, wq):
    T = x2.shape[0]
    half = PEER_KEY_DIM // 2
    row = lambda w: pl.BlockSpec((MERGE_TM, w), lambda i: (i, 0))
    full = lambda a: pl.BlockSpec(a.shape, lambda i: (0, 0))
    return pl.pallas_call(
        _merge_kernel,
        grid=(T // MERGE_TM,),
        in_specs=[row(att.shape[1]), row(hy.shape[1]),
                  pl.BlockSpec((MERGE_TM, 2 * D_MODEL), lambda i: (i, COL_GATE // (2 * D_MODEL))),
                  full(bg), row(D_MODEL), full(woa), full(woh), full(wout), full(fg), full(wq)],
        out_specs=[row(D_MODEL), row(D_MODEL),
                   pl.BlockSpec((QP_SLOTS, MERGE_TM, half), lambda i: (0, i, 0))],
        out_shape=[jax.ShapeDtypeStruct((T, D_MODEL), F32),
                   jax.ShapeDtypeStruct((T, D_MODEL), BF16),
                   jax.ShapeDtypeStruct((QP_SLOTS, T, half), BF16)],
        compiler_params=_cparams(("parallel",)),
        name="merge",
    )(att, hy, proj, bg, x2, woa, woh, wout, fg, wq)


ROUTE_TB = LANES
NEG_INF = float("-inf")
N_CHAINS = 2 * PEER_HEADS
CAND_ROWS = 72


def _extract_top(work_ref, vals_ref, n_chains):
    slot = lax.broadcasted_iota(jnp.int32, vals_ref.shape[1:], 0)

    def body(k, carry):
        for c in range(n_chains):
            s = work_ref[c]
            m = jnp.max(s, axis=0, keepdims=True)
            work_ref[c] = jnp.where(s == m, NEG_INF, s)
            vals_ref[c] = jnp.where(slot == k, m, vals_ref[c])
        return carry
    lax.fori_loop(0, PEER_TOPK, body, 0)


def _candidates(v1, v2):
    a16 = lax.broadcasted_iota(jnp.int32, v1.shape, 0)
    a8 = a16[0:8]
    return jnp.concatenate([
        v1[0:1] + v2,
        v1[1:2] + v2[0:8], v1[2:3] + v2[0:8], v1[3:4] + v2[0:8],
        jnp.where(a16 >= 4, v1 + v2[0:1], NEG_INF),
        jnp.where(a8 >= 4, v1[0:8] + v2[1:2], NEG_INF),
        jnp.where(a8 == 4, v1[0:8] + v2[2:3], NEG_INF),
    ], axis=0)


def _route_kernel(q_ref, k1_ref, k2_ref, cnt_ref, a_ref, rank_ref, b_ref,
                  s_ref, work_ref, vals_ref, cand_ref, cvals_ref):
    for h in range(PEER_HEADS):
        for side, k_ref in enumerate((k1_ref, k2_ref)):
            s = _dot_nt(k_ref[h], q_ref[2 * h + side])
            s_ref[2 * h + side] = s
            work_ref[2 * h + side] = s
    vals_ref[...] = jnp.zeros_like(vals_ref)
    _extract_top(work_ref, vals_ref, N_CHAINS)
    for h in range(PEER_HEADS):
        cand_ref[h] = _candidates(vals_ref[2 * h], vals_ref[2 * h + 1])
    cvals_ref[...] = jnp.zeros_like(cvals_ref)
    _extract_top(cand_ref, cvals_ref, PEER_HEADS)

    for h in range(PEER_HEADS):
        v1, v2, cv = vals_ref[2 * h], vals_ref[2 * h + 1], cvals_ref[h]
        s1, s2 = s_ref[2 * h], s_ref[2 * h + 1]
        tau = cv[PEER_TOPK - 1:PEER_TOPK]
        zsum = jnp.sum(jnp.exp(cv - cv[0:1]), axis=0, keepdims=True)
        cnt_sorted = jnp.zeros_like(v1)
        for b in range(PEER_TOPK):
            cnt_sorted = cnt_sorted + jnp.where(v1 + v2[b:b + 1] >= tau, 1.0, 0.0)
        cnt = jnp.zeros_like(s1)
        for a in range(PEER_TOPK):
            cnt = jnp.where(s1 == v1[a:a + 1], cnt_sorted[a:a + 1], cnt)
        rank = jnp.full_like(s2, float(PEER_TOPK))
        for b in reversed(range(PEER_TOPK)):
            rank = jnp.where(s2 >= v2[b:b + 1], float(b), rank)
        cnt_ref[0, h] = cnt * RANK_SCALE
        a_ref[0, h] = jnp.exp(s1 - v1[0:1]) * (GELU_SCALE / zsum)
        rank_ref[0, h] = (rank * RANK_SCALE).astype(BF16)
        b_ref[0, h] = jnp.exp(s2 - v2[0:1]).astype(BF16)


def _route(qp, k1, k2):
    T = qp.shape[1]
    half = PEER_KEY_DIM // 2
    blk = pl.BlockSpec((1, PEER_HEADS, N_KEYS, LANES), lambda i: (i, 0, 0, 0))
    shape = lambda dt: jax.ShapeDtypeStruct((T // LANES, PEER_HEADS, N_KEYS, LANES), dt)
    keys = pl.BlockSpec((PEER_HEADS, N_KEYS, half), lambda i: (0, 0, 0))
    return pl.pallas_call(
        _route_kernel,
        grid=(T // ROUTE_TB,),
        in_specs=[pl.BlockSpec((QP_SLOTS, ROUTE_TB, half), lambda i: (0, i, 0)), keys, keys],
        out_specs=[blk, blk, blk, blk],
        out_shape=[shape(F32), shape(F32), shape(BF16), shape(BF16)],
        scratch_shapes=[pltpu.VMEM((N_CHAINS, N_KEYS, ROUTE_TB), F32),
                        pltpu.VMEM((N_CHAINS, N_KEYS, ROUTE_TB), F32),
                        pltpu.VMEM((N_CHAINS, PEER_TOPK, ROUTE_TB), F32),
                        pltpu.VMEM((PEER_HEADS, CAND_ROWS, ROUTE_TB), F32),
                        pltpu.VMEM((PEER_HEADS, PEER_TOPK, ROUTE_TB), F32)],
        compiler_params=_cparams(("parallel",)),
        name="route",
    )(qp, k1, k2)


PEER_TB = 512
PEER_TE = 1024
PEER_ROWS = PEER_TE // N_KEYS
PEER_NC = PEER_TB // LANES
INV_SQRT2 = 1.0 / math.sqrt(2.0)
GELU_SCALE = 0.5 * math.sqrt(2.0)
RANK_SCALE = 256.0


def _row_tile(ref, c, h, ii):
    return jnp.broadcast_to(ref[c, h, ii:ii + 1, :], (N_KEYS, LANES)).astype(BF16)


def _peer_kernel(hn_ref, u_ref, vt_ref, cnt_ref, a_ref, rank_ref, b_ref, h_ref, o_ref, y_ref, act_ref, wt_ref):
    e = pl.program_id(1)

    @pl.when(e == 0)
    def _():
        y_ref[...] = jnp.zeros_like(y_ref)

    act_ref[...] = _dot_nt(u_ref[...], hn_ref[...])

    for half in range(PEER_NC // 2):
        for ii in range(PEER_ROWS):
            rs = slice(ii * N_KEYS, (ii + 1) * N_KEYS)
            for c in (2 * half, 2 * half + 1):
                ls = slice(c * LANES, (c + 1) * LANES)
                w = jnp.zeros((N_KEYS, LANES), BF16)
                for h in range(PEER_HEADS):
                    cnt = _row_tile(cnt_ref, c, h, ii)
                    wa = _row_tile(a_ref, c, h, ii)
                    sel = jnp.minimum(jnp.maximum(cnt - rank_ref[c, h], 0), b_ref[c, h])
                    w = w + sel * wa
                x = act_ref[rs, ls]
                wt_ref[rs, ls] = (x * (1.0 + lax.erf(x))).astype(BF16) * w
        hs = slice(half * 2 * LANES, (half + 1) * 2 * LANES)
        y_ref[:, hs] += _dot(vt_ref[...], wt_ref[:, hs])

    @pl.when(e == pl.num_programs(1) - 1)
    def _():
        o_ref[...] = h_ref[...] + y_ref[...].T


def _peer(hn, u, vt, cnt, a, rank, b, h):
    T = hn.shape[0]
    rows = pl.BlockSpec((PEER_NC, PEER_HEADS, PEER_ROWS, LANES), lambda t, e: (t, 0, e, 0))
    keys = pl.BlockSpec((PEER_NC, PEER_HEADS, N_KEYS, LANES), lambda t, e: (t, 0, 0, 0))
    tok = pl.BlockSpec((PEER_TB, D_MODEL), lambda t, e: (t, 0))
    return pl.pallas_call(
        _peer_kernel,
        grid=(T // PEER_TB, N_EXPERTS // PEER_TE),
        in_specs=[tok,
                  pl.BlockSpec((PEER_TE, D_MODEL), lambda t, e: (e, 0)),
                  pl.BlockSpec((D_MODEL, PEER_TE), lambda t, e: (0, e)),
                  rows, rows, keys, keys, tok],
        out_specs=tok,
        out_shape=jax.ShapeDtypeStruct((T, D_MODEL), F32),
        scratch_shapes=[pltpu.VMEM((D_MODEL, PEER_TB), F32),
                        pltpu.VMEM((PEER_TE, PEER_TB), F32),
                        pltpu.VMEM((PEER_TE, PEER_TB), BF16)],
        compiler_params=_cparams(("parallel", "arbitrary")),
        name="peer",
    )(hn, u, vt, cnt, a, rank, b, h)


def _regroup_w_in(w):
    o_kv = Q_LORA_RANK
    o_pe = o_kv + KV_LORA_RANK
    o_hy = o_pe + QK_ROPE_DIM
    o_gate = o_hy + 3 * HYENA_WIDTH
    out = jnp.zeros((D_MODEL, PROJ_COLS), BF16)
    out = out.at[:, COL_CQ:COL_CQ + Q_LORA_RANK].set(w[:, :o_kv].astype(BF16))
    out = out.at[:, COL_CKV:COL_CKV + KV_LORA_RANK].set(w[:, o_kv:o_pe].astype(BF16))
    out = out.at[:, COL_KPE + KPE_LANE:COL_KPE + KPE_LANE + QK_ROPE_DIM].set(w[:, o_pe:o_hy].astype(BF16))
    out = out.at[:, COL_HY:COL_HY + 3 * HYENA_WIDTH].set(w[:, o_hy:o_gate].astype(BF16))
    out = out.at[:, COL_GATE:].set(w[:, o_gate:].astype(BF16))
    return out


def _head_slots(w, width):
    k = w.shape[0]
    w3 = w.reshape(k, MLA_HEADS, width).astype(BF16)
    return jnp.zeros((k, MLA_HEADS, HEAD_SLOT), BF16).at[:, :, :width].set(w3).reshape(k, HW)


def kernel(x, attn_norm, w_in, b_gate, q_a_norm, w_uq, kv_a_norm, w_ukv, q_norm, k_norm, w_o_attn, hyena_conv_w, hyena_conv_b, filt_w1, filt_b1, filt_w2, filt_b2, filt_w3, filt_b3, filt_w4, filt_b4, filt_freq, hyena_bias, w_o_hyena, w_out, ffn_norm, peer_w_q, peer_keys1, peer_keys2, expert_u, expert_v):
    B = x.shape[0]
    T = B * SEQ
    x2 = x.reshape(T, D_MODEL)
    bf = lambda a: a.astype(BF16)

    proj = _inproj(x2, attn_norm, _regroup_w_in(w_in[0]))
    kv3 = w_ukv[0].reshape(KV_LORA_RANK, MLA_HEADS, QK_NOPE_DIM + V_HEAD_DIM)
    wuk_p = _head_slots(kv3[:, :, :QK_NOPE_DIM].reshape(KV_LORA_RANK, -1), QK_NOPE_DIM)
    wuv_p = _head_slots(kv3[:, :, QK_NOPE_DIM:].reshape(KV_LORA_RANK, -1), V_HEAD_DIM)
    vone = jnp.zeros((MLA_HEADS, HEAD_SLOT), F32).at[:, V_HEAD_DIM].set(1.0).reshape(1, HW)
    q, k, v = _mlaprep(proj, q_a_norm, kv_a_norm, _head_slots(w_uq[0], QK_HEAD_DIM), wuk_p, wuv_p,
                       _pad2(q_norm, 1, HEAD_SLOT), _pad2(k_norm, 1, HEAD_SLOT), vone)
    att = _attention(q, k, v, B).reshape(T, MLA_HEADS * V_HEAD_DIM)

    wf, wft = (jnp.asarray(w).astype(BF16) for w in _dft_tables())
    hsd = _filter(filt_w1[0], filt_b1[0], filt_w2[0], filt_b2[0], filt_w3[0], filt_b3[0], filt_w4[0],
                  filt_b4[0], filt_freq[0])
    kp = _kspec(wf, hsd)
    ys, p1, p2 = _hyfwd(proj.reshape(B, SEQ, PROJ_COLS), hyena_conv_w[0], hyena_conv_b, hyena_bias, wf, kp)
    hy = _hyinv(wft, ys, p1, p2).reshape(T, HYENA_WIDTH)

    h, hn, qp = _merge(att, hy, proj, b_gate, x2, bf(w_o_attn[0]), bf(w_o_hyena[0]), bf(w_out[0]),
                       ffn_norm, bf(peer_w_q[0]))

    cnt, a, rank, b = _route(qp, bf(peer_keys1[0]), bf(peer_keys2[0]))
    out = _peer(hn, bf(expert_u[0]), bf(expert_v[0]).T, cnt, a, rank, b, h)
    return out.reshape(B, SEQ, D_MODEL)
```

```python
import functools
import math

import numpy as np
import jax
import jax.numpy as jnp
from jax import lax
from jax.experimental import pallas as pl
from jax.experimental.pallas import tpu as pltpu

F32 = jnp.float32
BF16 = jnp.bfloat16

D_MODEL = 1024
SEQ = 2048
MLA_HEADS = 8
QK_NOPE_DIM = 64
QK_ROPE_DIM = 32
QK_HEAD_DIM = QK_NOPE_DIM + QK_ROPE_DIM
V_HEAD_DIM = 64
Q_LORA_RANK = 256
KV_LORA_RANK = 128
ROPE_THETA = 10000.0
HYENA_WIDTH = 512
FILTER_EMB = 33
FILTER_ORDER = 64
FAST_DECAY_PCT = 0.3
SLOW_DECAY_PCT = 1.5
DECAY_TARGET = 1e-2
PEER_HEADS = 8
N_KEYS = 128
N_EXPERTS = N_KEYS * N_KEYS
PEER_KEY_DIM = 256
PEER_TOPK = 16
EPS = 1e-6

LANES = 128
HEAD_SLOT = LANES
FFT_N = 2 * SEQ
PROJ_COLS = 4096
COL_CQ, COL_CKV, COL_KPE, COL_HY, COL_GATE = 0, 256, 384, 512, 2048
KPE_LANE = QK_NOPE_DIM
VMEM_LIMIT = 56 * 1024 * 1024


def _cparams(sem):
    return pltpu.CompilerParams(dimension_semantics=sem, vmem_limit_bytes=VMEM_LIMIT)


def _dot(a, b):
    return jnp.dot(a, b, preferred_element_type=F32)


def _dot_nt(a, b):
    return lax.dot_general(a, b, (((1,), (1,)), ((), ())), preferred_element_type=F32)


def _rms(x, g):
    return x * lax.rsqrt(jnp.mean(x * x, axis=-1, keepdims=True) + EPS) * g


@functools.lru_cache(maxsize=None)
def _dft_tables():
    p = np.arange(FFT_N, dtype=np.int64)[:, None]
    s = np.arange(SEQ, dtype=np.int64)[None, :]
    f = np.where(p <= SEQ, p, p - SEQ)
    ang = (2.0 * np.pi / FFT_N) * ((f * s) % FFT_N).astype(np.float64)
    w = np.where(p <= SEQ, np.cos(ang), np.sin(ang)).astype(np.float32)
    return w, np.ascontiguousarray(w.T)


@functools.lru_cache(maxsize=None)
def _rotary_tables():
    half = QK_ROPE_DIM // 2
    inv_freq = ROPE_THETA ** (-np.arange(half, dtype=np.float64) / half)
    ang = np.arange(SEQ, dtype=np.float64)[:, None] * inv_freq[None, :]
    cos = np.ones((SEQ, HEAD_SLOT)); sa = np.zeros((SEQ, HEAD_SLOT)); sb = np.zeros((SEQ, HEAD_SLOT))
    lo, mid, hi = QK_NOPE_DIM, QK_NOPE_DIM + half, QK_HEAD_DIM
    cos[:, lo:mid] = np.cos(ang); cos[:, mid:hi] = np.cos(ang)
    sb[:, lo:mid] = -np.sin(ang)
    sa[:, mid:hi] = np.sin(ang)
    return tuple(jnp.asarray(t, dtype=F32) for t in (cos, sa, sb))


@functools.lru_cache(maxsize=None)
def _filter_consts():
    L = SEQ
    t = np.linspace(0.0, 1.0, L)[:, None]
    bands = (FILTER_EMB - 1) // 2
    w = 2.0 * math.pi * np.arange(L)[:, None] / L
    f = np.linspace(1e-4, bands - 1, bands)[None, :]
    z = np.concatenate([t, np.cos(f * w), -np.sin(f * w)], axis=-1)
    zp = np.zeros((L, LANES)); zp[:, :FILTER_EMB] = z
    min_decay = math.log(DECAY_TARGET) / SLOW_DECAY_PCT
    max_decay = math.log(DECAY_TARGET) / FAST_DECAY_PCT
    deltas = np.abs(np.linspace(min_decay, max_decay, HYENA_WIDTH))[None, :]
    return jnp.asarray(zp, dtype=F32), jnp.asarray(deltas, dtype=F32)


INPROJ_TM = 512
INPROJ_CW = 1024


def _inproj_kernel(x_ref, g_ref, w_ref, o_ref):
    xn = _rms(x_ref[...], g_ref[...]).astype(BF16)
    for c in range(PROJ_COLS // INPROJ_CW):
        sl = slice(c * INPROJ_CW, (c + 1) * INPROJ_CW)
        o_ref[:, sl] = _dot(xn, w_ref[:, sl]).astype(BF16)


def _inproj(x2, g, w_p):
    T = x2.shape[0]
    return pl.pallas_call(
        _inproj_kernel,
        grid=(T // INPROJ_TM,),
        in_specs=[pl.BlockSpec((INPROJ_TM, D_MODEL), lambda i: (i, 0)),
                  pl.BlockSpec((1, D_MODEL), lambda i: (0, 0)),
                  pl.BlockSpec((D_MODEL, PROJ_COLS), lambda i: (0, 0))],
        out_specs=pl.BlockSpec((INPROJ_TM, PROJ_COLS), lambda i: (i, 0)),
        out_shape=jax.ShapeDtypeStruct((T, PROJ_COLS), BF16),
        compiler_params=_cparams(("parallel",)),
        name="inproj",
    )(x2, g, w_p)


MLA_TM = 512
HW = MLA_HEADS * HEAD_SLOT


def _rope_partner(a, axis):
    half = QK_ROPE_DIM // 2
    lo, mid, hi = QK_NOPE_DIM, QK_NOPE_DIM + half, QK_HEAD_DIM
    take = lambda s, e: lax.slice_in_dim(a, s, e, axis=axis)
    pad = [(0, 0)] * a.ndim
    pad[axis] = (lo, a.shape[axis] - hi)
    return jnp.pad(jnp.concatenate([take(mid, hi), take(lo, mid)], axis=axis), pad)


def _mlaprep_kernel(p_ref, qa_ref, kva_ref, wuq_ref, wuqr_ref, wuk_ref, wuv_ref, gq_ref, gqr_ref, gk_ref,
                    vone_ref, ones_ref, cos_ref, sa_ref, sb_ref, q_ref, k_ref, v_ref):
    p = p_ref[...].astype(F32)
    cqn = _rms(p[:, COL_CQ:COL_CQ + Q_LORA_RANK], qa_ref[...]).astype(BF16)
    ckvn = _rms(p[:, COL_CKV:COL_CKV + KV_LORA_RANK], kva_ref[...]).astype(BF16)
    kpe = p[:, COL_KPE:COL_KPE + HEAD_SLOT]
    q = _dot(cqn, wuq_ref[...])
    qr = _dot(cqn, wuqr_ref[...])
    kn = _dot(ckvn, wuk_ref[...])
    v_ref[...] = (_dot(ckvn, wuv_ref[...]) + vone_ref[...]).astype(BF16)
    ones = ones_ref[...]
    pair = 2 * HEAD_SLOT

    def head_sums(x):
        sq = (x * x).astype(BF16)
        return jnp.concatenate([_dot(sq[:, i:i + pair], ones) for i in range(0, HW, pair)], axis=1)
    ssq = head_sums(q)
    ssk = head_sums(kn)
    sspe = _dot((kpe * kpe).astype(BF16), ones_ref[:HEAD_SLOT, :HEAD_SLOT])
    cos, sa, sb = cos_ref[...], sa_ref[...], sb_ref[...]
    scale = QK_HEAD_DIM ** -0.5 * math.log2(math.e)
    inv_d = 1.0 / QK_HEAD_DIM
    tq = cos * (gq_ref[...] * scale)
    tqr = (sa + sb) * (gqr_ref[...] * scale)
    tk = cos * gk_ref[...]
    kg = kpe * gk_ref[...]
    half = QK_ROPE_DIM // 2
    rk = pltpu.roll(kg, half, 1) * sa + pltpu.roll(kg, HEAD_SLOT - half, 1) * sb
    for h in range(MLA_HEADS):
        sl = slice(h * HEAD_SLOT, (h + 1) * HEAD_SLOT)
        inv_q = lax.rsqrt(ssq[:, sl] * inv_d + EPS)
        q_ref[:, sl] = ((q[:, sl] * tq + qr[:, sl] * tqr) * inv_q).astype(BF16)
        inv_k = lax.rsqrt((ssk[:, sl] + sspe) * inv_d + EPS)
        k_ref[:, sl] = (((kn[:, sl] + kpe) * tk + rk) * inv_k).astype(BF16)


def _mlaprep(proj, qa, kva, wuq_p, wuqr_p, wuk_p, wuv_p, gq, gqr, gk, vone):
    T = proj.shape[0]
    cos, sa, sb = _rotary_tables()
    head_of = np.arange(2 * HEAD_SLOT) // HEAD_SLOT
    ones = jnp.asarray(head_of[:, None] == head_of[None, :], dtype=BF16)
    nseq = SEQ // MLA_TM
    full = lambda shape: pl.BlockSpec(shape, lambda i: (0, 0))
    pos = pl.BlockSpec((MLA_TM, HEAD_SLOT), lambda i: (i % nseq, 0))
    out = pl.BlockSpec((MLA_TM, HW), lambda i: (i, 0))
    return pl.pallas_call(
        _mlaprep_kernel,
        grid=(T // MLA_TM,),
        in_specs=[pl.BlockSpec((MLA_TM, 512), lambda i: (i, 0)),
                  full((1, Q_LORA_RANK)), full((1, KV_LORA_RANK)),
                  full((Q_LORA_RANK, HW)), full((Q_LORA_RANK, HW)), full((KV_LORA_RANK, HW)),
                  full((KV_LORA_RANK, HW)),
                  full((1, HEAD_SLOT)), full((1, HEAD_SLOT)), full((1, HEAD_SLOT)), full((1, HW)),
                  full((2 * HEAD_SLOT, 2 * HEAD_SLOT)), pos, pos, pos],
        out_specs=[out, out, out],
        out_shape=[jax.ShapeDtypeStruct((T, HW), BF16)] * 3,
        compiler_params=_cparams(("parallel",)),
        name="mlaprep",
    )(proj, qa, kva, wuq_p, wuqr_p, wuk_p, wuv_p, gq, gqr, gk, vone, ones, cos, sa, sb)


ATT_TQ = 512


def _attn_kernel(q_ref, k_ref, v_ref, o_ref):
    outs = []
    for hh in range(2):
        sl = slice(hh * HEAD_SLOT, (hh + 1) * HEAD_SLOT)
        s = _dot_nt(q_ref[0, :, sl], k_ref[0, :, sl])
        m = jnp.max(s, axis=-1, keepdims=True)
        p = jnp.exp2(s - m).astype(BF16)
        pv = _dot(p, v_ref[0, :, sl])
        outs.append(pv[:, :V_HEAD_DIM] / pv[:, V_HEAD_DIM:V_HEAD_DIM + 1])
    o_ref[0] = jnp.concatenate(outs, axis=-1).astype(BF16)


def _attention(q, k, v, B):
    q3, k3, v3 = (a.reshape(B, SEQ, HW) for a in (q, k, v))
    pair = 2 * HEAD_SLOT
    return pl.pallas_call(
        _attn_kernel,
        grid=(B, MLA_HEADS // 2, SEQ // ATT_TQ),
        in_specs=[pl.BlockSpec((1, ATT_TQ, pair), lambda b, h, i: (b, i, h)),
                  pl.BlockSpec((1, SEQ, pair), lambda b, h, i: (b, 0, h)),
                  pl.BlockSpec((1, SEQ, pair), lambda b, h, i: (b, 0, h))],
        out_specs=pl.BlockSpec((1, ATT_TQ, 2 * V_HEAD_DIM), lambda b, h, i: (b, i, h)),
        out_shape=jax.ShapeDtypeStruct((B, SEQ, MLA_HEADS * V_HEAD_DIM), BF16),
        compiler_params=_cparams(("parallel", "parallel", "parallel")),
        name="attn",
    )(q3, k3, v3)


FILT_TL = 512


def _filter_kernel(z_ref, w1_ref, b1_ref, w2_ref, b2_ref, w3_ref, b3_ref, w4_ref, b4_ref, fr_ref,
                   dl_ref, o_ref):
    hp = functools.partial(jnp.dot, precision=lax.Precision.HIGHEST, preferred_element_type=F32)
    z = z_ref[...]
    fr = fr_ref[...]
    h = jnp.sin(fr * (hp(z, w1_ref[...]) + b1_ref[...]))
    h = jnp.sin(fr * (hp(h, w2_ref[...]) + b2_ref[...]))
    h = jnp.sin(fr * (hp(h, w3_ref[...]) + b3_ref[...]))
    h4 = hp(h, w4_ref[...]) + b4_ref[...]
    decay = jnp.exp(-z[:, 0:1] * dl_ref[...])
    hf = h4[:, :HYENA_WIDTH] * decay
    hb = h4[:, HYENA_WIDTH:] * decay
    row = pl.program_id(0) * FILT_TL + lax.broadcasted_iota(jnp.int32, hb.shape, 0)
    hb = jnp.where(row == 0, 0.0, hb)
    o_ref[:, :HYENA_WIDTH] = (hf + hb).astype(BF16)
    o_ref[:, HYENA_WIDTH:] = (hf - hb).astype(BF16)


def _pad2(a, r, c):
    return jnp.zeros((r, c), a.dtype).at[:a.shape[0], :a.shape[1]].set(a)


def _filter(w1, b1, w2, b2, w3, b3, w4, b4, freq):
    z, deltas = _filter_consts()
    P = LANES
    args = (z, _pad2(w1, P, P), _pad2(b1[None], 1, P), _pad2(w2, P, P), _pad2(b2[None], 1, P),
            _pad2(w3, P, P), _pad2(b3[None], 1, P), _pad2(w4, P, 2 * HYENA_WIDTH), b4[None],
            _pad2(freq[None], 1, P), deltas)
    full = lambda a: pl.BlockSpec(a.shape, lambda i: (0, 0))
    return pl.pallas_call(
        _filter_kernel,
        grid=(SEQ // FILT_TL,),
        in_specs=[pl.BlockSpec((FILT_TL, P), lambda i: (i, 0))] + [full(a) for a in args[1:]],
        out_specs=pl.BlockSpec((FILT_TL, 2 * HYENA_WIDTH), lambda i: (i, 0)),
        out_shape=jax.ShapeDtypeStruct((SEQ, 2 * HYENA_WIDTH), BF16),
        compiler_params=_cparams(("parallel",)),
        name="filt",
    )(*args)


SPEC_TR = 512


def _kspec_kernel(w_ref, h_ref, o_ref):
    r = _dot(w_ref[...], h_ref[...])
    row = pl.program_id(0) * SPEC_TR + lax.broadcasted_iota(jnp.int32, (SPEC_TR, HYENA_WIDTH), 0)
    o_ref[...] = jnp.where(row <= SEQ, r[:, :HYENA_WIDTH], r[:, HYENA_WIDTH:])


def _kspec(wf, hsd):
    return pl.pallas_call(
        _kspec_kernel,
        grid=(FFT_N // SPEC_TR,),
        in_specs=[pl.BlockSpec((SPEC_TR, SEQ), lambda i: (i, 0)),
                  pl.BlockSpec((SEQ, 2 * HYENA_WIDTH), lambda i: (0, 0))],
        out_specs=pl.BlockSpec((SPEC_TR, HYENA_WIDTH), lambda i: (i, 0)),
        out_shape=jax.ShapeDtypeStruct((FFT_N, HYENA_WIDTH), F32),
        compiler_params=_cparams(("parallel",)),
        name="kspec",
    )(wf, hsd)


def _short_conv(u, w, b):
    row = lax.broadcasted_iota(jnp.int32, u.shape, 0)
    prev = jnp.where(row == 0, 0.0, pltpu.roll(u, 1, 0))
    nxt = jnp.where(row == SEQ - 1, 0.0, pltpu.roll(u, SEQ - 1, 0))
    return prev * w[0:1] + u * w[1:2] + nxt * w[2:3] + b


def _hyfwd_kernel(x0_ref, x1_ref, v_ref, cw_ref, cb_ref, bias_ref, wre_ref, wim_ref, kre_ref, kim_ref,
                  y_ref, p1_ref, p2_ref, z_ref):
    r = pl.program_id(1)
    C = HYENA_WIDTH

    @pl.when(r == 0)
    def _():
        cw = cw_ref[...]
        cb = cb_ref[...]
        x0 = _short_conv(x0_ref[0].astype(F32), cw[:, 0:C], cb[:, 0:C])
        x1 = _short_conv(x1_ref[0].astype(F32), cw[:, C:2 * C], cb[:, C:2 * C])
        v = _short_conv(v_ref[0].astype(F32), cw[:, 2 * C:], cb[:, 2 * C:])
        z = v * x1
        z_ref[...] = z.astype(BF16)
        p1_ref[0] = x0.astype(BF16)
        p2_ref[0] = (x0 * z * bias_ref[...]).astype(BF16)

    z = z_ref[...]
    a = _dot(wre_ref[...], z)
    q = _dot(wim_ref[...], z)
    ka = kre_ref[...]
    kq = kim_ref[...]
    first = jnp.logical_and(r == 0, lax.broadcasted_iota(jnp.int32, a.shape, 0) == 0)
    yr = jnp.where(first, a * ka, a * ka - q * kq)
    yq = jnp.where(first, q * kq, a * kq + q * ka)
    sc = jnp.where(first, 1.0 / FFT_N, 2.0 / FFT_N)
    y_ref[0, 0] = (yr * sc).astype(BF16)
    y_ref[0, 1] = (yq * sc).astype(BF16)


HY_TR = 512


def _hyfwd(proj3, cw, cb, bias, wf, kp):
    B = proj3.shape[0]
    nr = SEQ // HY_TR
    cblk = lambda c: pl.BlockSpec((1, SEQ, HYENA_WIDTH), lambda b, r: (b, 0, c))
    full = lambda a: pl.BlockSpec(a.shape, lambda b, r: (0, 0))
    seq_out = pl.BlockSpec((1, SEQ, HYENA_WIDTH), lambda b, r: (b, 0, 0))
    y, p1, p2 = pl.pallas_call(
        _hyfwd_kernel,
        grid=(B, nr),
        in_specs=[cblk(COL_HY // HYENA_WIDTH), cblk(COL_HY // HYENA_WIDTH + 1), cblk(COL_HY // HYENA_WIDTH + 2),
                  full(cw), full(cb), full(bias),
                  pl.BlockSpec((HY_TR, SEQ), lambda b, r: (r, 0)),
                  pl.BlockSpec((HY_TR, SEQ), lambda b, r: (r + nr, 0)),
                  pl.BlockSpec((HY_TR, HYENA_WIDTH), lambda b, r: (r, 0)),
                  pl.BlockSpec((HY_TR, HYENA_WIDTH), lambda b, r: (r + nr, 0))],
        out_specs=[pl.BlockSpec((1, 2, HY_TR, HYENA_WIDTH), lambda b, r: (b, 0, r, 0)), seq_out, seq_out],
        out_shape=[jax.ShapeDtypeStruct((B, 2, SEQ, HYENA_WIDTH), BF16),
                   jax.ShapeDtypeStruct((B, SEQ, HYENA_WIDTH), BF16),
                   jax.ShapeDtypeStruct((B, SEQ, HYENA_WIDTH), BF16)],
        scratch_shapes=[pltpu.VMEM((SEQ, HYENA_WIDTH), BF16)],
        compiler_params=_cparams(("parallel", "arbitrary")),
        name="hyfwd",
    )(proj3, proj3, proj3, cw, cb, bias, wf, wf, kp, kp)
    return y.reshape(B, FFT_N, HYENA_WIDTH), p1, p2


HYI_TT = 512


def _hyinv_kernel(wt_ref, y_ref, p1_ref, p2_ref, o_ref):
    conv = _dot(wt_ref[...], y_ref[0])
    o_ref[0] = (p1_ref[0].astype(F32) * conv + p2_ref[0].astype(F32)).astype(BF16)


def _hyinv(wft, ys, p1, p2):
    B = ys.shape[0]
    blk = pl.BlockSpec((1, HYI_TT, HYENA_WIDTH), lambda t, b: (b, t, 0))
    return pl.pallas_call(
        _hyinv_kernel,
        grid=(SEQ // HYI_TT, B),
        in_specs=[pl.BlockSpec((HYI_TT, FFT_N), lambda t, b: (t, 0)),
                  pl.BlockSpec((1, FFT_N, HYENA_WIDTH), lambda t, b: (b, 0, 0)),
                  blk, blk],
        out_specs=blk,
        out_shape=jax.ShapeDtypeStruct((B, SEQ, HYENA_WIDTH), BF16),
        compiler_params=_cparams(("parallel", "parallel")),
        name="hyinv",
    )(wft, ys, p1, p2)


MERGE_TM = 512
QP_SLOTS = PEER_HEADS * 2


def _merge_kernel(att_ref, hy_ref, gl_ref, bg_ref, x_ref, woa_ref, woh_ref, wout_ref, fg_ref, wq_ref,
                  h_ref, hn_ref, qp_ref):
    a = _dot(att_ref[...], woa_ref[...])
    yh = _dot(hy_ref[...], woh_ref[...])
    g = 1.0 / (1.0 + jnp.exp(-(gl_ref[...].astype(F32) + bg_ref[...])))
    merged = (g[:, :D_MODEL] * a + g[:, D_MODEL:] * yh).astype(BF16)
    h = x_ref[...] + _dot(merged, wout_ref[...])
    h_ref[...] = h
    hn = _rms(h, fg_ref[...])
    hn_ref[...] = (hn * INV_SQRT2).astype(BF16)
    qp = _dot(hn.astype(BF16), wq_ref[...])
    for c in range(QP_SLOTS):
        qp_ref[c] = qp[:, c * LANES:(c + 1) * LANES].astype(BF16)


def _merge(att, hy, proj, bg, x2, woa, woh, wout, fg, wq):
    T = x2.shape[0]
    half = PEER_KEY_DIM // 2
    row = lambda w: pl.BlockSpec((MERGE_TM, w), lambda i: (i, 0))
    full = lambda a: pl.BlockSpec(a.shape, lambda i: (0, 0))
    return pl.pallas_call(
        _merge_kernel,
        grid=(T // MERGE_TM,),
        in_specs=[row(att.shape[1]), row(hy.shape[1]),
                  pl.BlockSpec((MERGE_TM, 2 * D_MODEL), lambda i: (i, COL_GATE // (2 * D_MODEL))),
                  full(bg), row(D_MODEL), full(woa), full(woh), full(wout), full(fg), full(wq)],
        out_specs=[row(D_MODEL), row(D_MODEL),
                   pl.BlockSpec((QP_SLOTS, MERGE_TM, half), lambda i: (0, i, 0))],
        out_shape=[jax.ShapeDtypeStruct((T, D_MODEL), F32),
                   jax.ShapeDtypeStruct((T, D_MODEL), BF16),
                   jax.ShapeDtypeStruct((QP_SLOTS, T, half), BF16)],
        compiler_params=_cparams(("parallel",)),
        name="merge",
    )(att, hy, proj, bg, x2, woa, woh, wout, fg, wq)


ROUTE_TB = LANES
NEG_INF = float("-inf")
N_CHAINS = 2 * PEER_HEADS
CAND_ROWS = 72


def _extract_top(work_ref, vals_ref, n_chains):
    slot = lax.broadcasted_iota(jnp.int32, vals_ref.shape[1:], 0)

    def body(k, carry):
        for c in range(n_chains):
            s = work_ref[c]
            m = jnp.max(s, axis=0, keepdims=True)
            work_ref[c] = jnp.where(s == m, NEG_INF, s)
            vals_ref[c] = jnp.where(slot == k, m, vals_ref[c])
        return carry
    lax.fori_loop(0, PEER_TOPK, body, 0)


def _candidates(v1, v2):
    a16 = lax.broadcasted_iota(jnp.int32, v1.shape, 0)
    a8 = a16[0:8]
    return jnp.concatenate([
        v1[0:1] + v2,
        v1[1:2] + v2[0:8], v1[2:3] + v2[0:8], v1[3:4] + v2[0:8],
        jnp.where(a16 >= 4, v1 + v2[0:1], NEG_INF),
        jnp.where(a8 >= 4, v1[0:8] + v2[1:2], NEG_INF),
        jnp.where(a8 == 4, v1[0:8] + v2[2:3], NEG_INF),
    ], axis=0)


def _sort_pairs(n):
    pairs = []

    def merge(lo, hi, r):
        step = 2 * r
        if step < hi - lo:
            merge(lo, hi, step)
            merge(lo + r, hi, step)
            pairs.extend((i, i + r) for i in range(lo + r, hi - r, step))
        else:
            pairs.append((lo, lo + r))

    def sort(lo, hi):
        if hi > lo:
            mid = lo + (hi - lo) // 2
            sort(lo, mid)
            sort(mid + 1, hi)
            merge(lo, hi, 1)
    sort(0, n - 1)
    return pairs


SUBLANES = 8


def _top_sorted(s):
    n = s.shape[0] // SUBLANES
    v = [s[SUBLANES * i:SUBLANES * (i + 1), :] for i in range(n)]
    for i, j in _sort_pairs(n):
        v[i], v[j] = jnp.maximum(v[i], v[j]), jnp.minimum(v[i], v[j])
    out = []
    for k in range(PEER_TOPK):
        m = jnp.max(v[0], axis=0, keepdims=True)
        out.append(m)
        if k + 1 < PEER_TOPK:
            popped = v[0] == m
            for d in range(PEER_TOPK - 1 - k):
                v[d] = jnp.where(popped, v[d + 1], v[d])
    return jnp.concatenate(out, axis=0)


def _route_kernel(q_ref, k1_ref, k2_ref, cnt_ref, a_ref, rank_ref, b_ref,
                  s_ref, vals_ref, cand_ref, cvals_ref):
    for h in range(PEER_HEADS):
        for side, k_ref in enumerate((k1_ref, k2_ref)):
            s = _dot_nt(k_ref[h], q_ref[2 * h + side])
            s_ref[2 * h + side] = s
            vals_ref[2 * h + side] = _top_sorted(s)
    for h in range(PEER_HEADS):
        cand_ref[h] = _candidates(vals_ref[2 * h], vals_ref[2 * h + 1])
    cvals_ref[...] = jnp.zeros_like(cvals_ref)
    _extract_top(cand_ref, cvals_ref, PEER_HEADS)

    for h in range(PEER_HEADS):
        v1, v2, cv = vals_ref[2 * h], vals_ref[2 * h + 1], cvals_ref[h]
        s1, s2 = s_ref[2 * h], s_ref[2 * h + 1]
        tau = cv[PEER_TOPK - 1:PEER_TOPK]
        zsum = jnp.sum(jnp.exp(cv - cv[0:1]), axis=0, keepdims=True)
        cnt_sorted = jnp.zeros_like(v1)
        for b in range(PEER_TOPK):
            cnt_sorted = cnt_sorted + jnp.where(v1 + v2[b:b + 1] >= tau, 1.0, 0.0)
        cnt = jnp.zeros_like(s1)
        for a in range(PEER_TOPK):
            cnt = jnp.where(s1 == v1[a:a + 1], cnt_sorted[a:a + 1], cnt)
        rank = jnp.full_like(s2, float(PEER_TOPK))
        for b in reversed(range(PEER_TOPK)):
            rank = jnp.where(s2 >= v2[b:b + 1], float(b), rank)
        cnt_ref[0, h] = cnt * RANK_SCALE
        a_ref[0, h] = jnp.exp(s1 - v1[0:1]) * (GELU_SCALE / zsum)
        rank_ref[0, h] = (rank * RANK_SCALE).astype(BF16)
        b_ref[0, h] = jnp.exp(s2 - v2[0:1]).astype(BF16)


def _route(qp, k1, k2):
    T = qp.shape[1]
    half = PEER_KEY_DIM // 2
    blk = pl.BlockSpec((1, PEER_HEADS, N_KEYS, LANES), lambda i: (i, 0, 0, 0))
    shape = lambda dt: jax.ShapeDtypeStruct((T // LANES, PEER_HEADS, N_KEYS, LANES), dt)
    keys = pl.BlockSpec((PEER_HEADS, N_KEYS, half), lambda i: (0, 0, 0))
    return pl.pallas_call(
        _route_kernel,
        grid=(T // ROUTE_TB,),
        in_specs=[pl.BlockSpec((QP_SLOTS, ROUTE_TB, half), lambda i: (0, i, 0)), keys, keys],
        out_specs=[blk, blk, blk, blk],
        out_shape=[shape(F32), shape(F32), shape(BF16), shape(BF16)],
        scratch_shapes=[pltpu.VMEM((N_CHAINS, N_KEYS, ROUTE_TB), F32),
                        pltpu.VMEM((N_CHAINS, PEER_TOPK, ROUTE_TB), F32),
                        pltpu.VMEM((PEER_HEADS, CAND_ROWS, ROUTE_TB), F32),
                        pltpu.VMEM((PEER_HEADS, PEER_TOPK, ROUTE_TB), F32)],
        compiler_params=_cparams(("parallel",)),
        name="route",
    )(qp, k1, k2)


PEER_TB = 512
PEER_TE = 2048
PEER_ROWS = PEER_TE // N_KEYS
PEER_NC = PEER_TB // LANES
INV_SQRT2 = 1.0 / math.sqrt(2.0)
GELU_SCALE = 0.5 * math.sqrt(2.0)
RANK_SCALE = 256.0


def _row_tile(ref, c, h, ii):
    return jnp.broadcast_to(ref[c, h, ii:ii + 1, :], (N_KEYS, LANES)).astype(BF16)


def _peer_kernel(hn_ref, u_ref, vt_ref, cnt_ref, a_ref, rank_ref, b_ref, h_ref, o_ref, y_ref, act_ref, wt_ref):
    e = pl.program_id(1)

    @pl.when(e == 0)
    def _():
        y_ref[...] = jnp.zeros_like(y_ref)

    act = _dot_nt(u_ref[...], hn_ref[...])
    for c in range(PEER_NC):
        act_ref[c] = act[:, c * LANES:(c + 1) * LANES]

    def chunk(c, carry):
        for ii in range(PEER_ROWS):
            rs = slice(ii * N_KEYS, (ii + 1) * N_KEYS)
            w = jnp.zeros((N_KEYS, LANES), BF16)
            for h in range(PEER_HEADS):
                cnt = _row_tile(cnt_ref, c, h, ii)
                wa = _row_tile(a_ref, c, h, ii)
                sel = jnp.minimum(jnp.maximum(cnt - rank_ref[c, h], 0), b_ref[c, h])
                w = w + sel * wa
            x = act_ref[c, rs, :]
            wt_ref[c, rs, :] = (x * (1.0 + lax.erf(x))).astype(BF16) * w
        return carry
    lax.fori_loop(0, PEER_NC, chunk, 0)

    wt = jnp.concatenate([wt_ref[c] for c in range(PEER_NC)], axis=1)
    y_ref[...] += _dot(vt_ref[...], wt)

    @pl.when(e == pl.num_programs(1) - 1)
    def _():
        o_ref[...] = h_ref[...] + y_ref[...].T


def _peer(hn, u, vt, cnt, a, rank, b, h):
    T = hn.shape[0]
    rows = pl.BlockSpec((PEER_NC, PEER_HEADS, PEER_ROWS, LANES), lambda t, e: (t, 0, e, 0))
    keys = pl.BlockSpec((PEER_NC, PEER_HEADS, N_KEYS, LANES), lambda t, e: (t, 0, 0, 0))
    tok = pl.BlockSpec((PEER_TB, D_MODEL), lambda t, e: (t, 0))
    return pl.pallas_call(
        _peer_kernel,
        grid=(T // PEER_TB, N_EXPERTS // PEER_TE),
        in_specs=[tok,
                  pl.BlockSpec((PEER_TE, D_MODEL), lambda t, e: (e, 0)),
                  pl.BlockSpec((D_MODEL, PEER_TE), lambda t, e: (0, e)),
                  rows, rows, keys, keys, tok],
        out_specs=tok,
        out_shape=jax.ShapeDtypeStruct((T, D_MODEL), F32),
        scratch_shapes=[pltpu.VMEM((D_MODEL, PEER_TB), F32),
                        pltpu.VMEM((PEER_NC, PEER_TE, LANES), F32),
                        pltpu.VMEM((PEER_NC, PEER_TE, LANES), BF16)],
        compiler_params=_cparams(("parallel", "arbitrary")),
        name="peer",
    )(hn, u, vt, cnt, a, rank, b, h)


def _regroup_w_in(w):
    o_kv = Q_LORA_RANK
    o_pe = o_kv + KV_LORA_RANK
    o_hy = o_pe + QK_ROPE_DIM
    o_gate = o_hy + 3 * HYENA_WIDTH
    out = jnp.zeros((D_MODEL, PROJ_COLS), BF16)
    out = out.at[:, COL_CQ:COL_CQ + Q_LORA_RANK].set(w[:, :o_kv].astype(BF16))
    out = out.at[:, COL_CKV:COL_CKV + KV_LORA_RANK].set(w[:, o_kv:o_pe].astype(BF16))
    out = out.at[:, COL_KPE + KPE_LANE:COL_KPE + KPE_LANE + QK_ROPE_DIM].set(w[:, o_pe:o_hy].astype(BF16))
    out = out.at[:, COL_HY:COL_HY + 3 * HYENA_WIDTH].set(w[:, o_hy:o_gate].astype(BF16))
    out = out.at[:, COL_GATE:].set(w[:, o_gate:].astype(BF16))
    return out


def _head_slots(w, width):
    k = w.shape[0]
    w3 = w.reshape(k, MLA_HEADS, width).astype(BF16)
    return jnp.zeros((k, MLA_HEADS, HEAD_SLOT), BF16).at[:, :, :width].set(w3).reshape(k, HW)


def kernel(x, attn_norm, w_in, b_gate, q_a_norm, w_uq, kv_a_norm, w_ukv, q_norm, k_norm, w_o_attn, hyena_conv_w, hyena_conv_b, filt_w1, filt_b1, filt_w2, filt_b2, filt_w3, filt_b3, filt_w4, filt_b4, filt_freq, hyena_bias, w_o_hyena, w_out, ffn_norm, peer_w_q, peer_keys1, peer_keys2, expert_u, expert_v):
    B = x.shape[0]
    T = B * SEQ
    x2 = x.reshape(T, D_MODEL)
    bf = lambda a: a.astype(BF16)

    proj = _inproj(x2, attn_norm, _regroup_w_in(w_in[0]))
    kv3 = w_ukv[0].reshape(KV_LORA_RANK, MLA_HEADS, QK_NOPE_DIM + V_HEAD_DIM)
    wuk_p = _head_slots(kv3[:, :, :QK_NOPE_DIM].reshape(KV_LORA_RANK, -1), QK_NOPE_DIM)
    wuv_p = _head_slots(kv3[:, :, QK_NOPE_DIM:].reshape(KV_LORA_RANK, -1), V_HEAD_DIM)
    vone = jnp.zeros((MLA_HEADS, HEAD_SLOT), F32).at[:, V_HEAD_DIM].set(1.0).reshape(1, HW)
    wuq_p = _head_slots(w_uq[0], QK_HEAD_DIM)
    wuqr_p = _rope_partner(wuq_p.reshape(Q_LORA_RANK, MLA_HEADS, HEAD_SLOT), axis=2).reshape(Q_LORA_RANK, HW)
    gq = _pad2(q_norm, 1, HEAD_SLOT)
    q, k, v = _mlaprep(proj, q_a_norm, kv_a_norm, wuq_p, wuqr_p, wuk_p, wuv_p,
                       gq, _rope_partner(gq, axis=1), _pad2(k_norm, 1, HEAD_SLOT), vone)
    att = _attention(q, k, v, B).reshape(T, MLA_HEADS * V_HEAD_DIM)

    wf, wft = (jnp.asarray(w).astype(BF16) for w in _dft_tables())
    hsd = _filter(filt_w1[0], filt_b1[0], filt_w2[0], filt_b2[0], filt_w3[0], filt_b3[0], filt_w4[0],
                  filt_b4[0], filt_freq[0])
    kp = _kspec(wf, hsd)
    ys, p1, p2 = _hyfwd(proj.reshape(B, SEQ, PROJ_COLS), hyena_conv_w[0], hyena_conv_b, hyena_bias, wf, kp)
    hy = _hyinv(wft, ys, p1, p2).reshape(T, HYENA_WIDTH)

    h, hn, qp = _merge(att, hy, proj, b_gate, x2, bf(w_o_attn[0]), bf(w_o_hyena[0]), bf(w_out[0]),
                       ffn_norm, bf(peer_w_q[0]))

    cnt, a, rank, b = _route(qp, bf(peer_keys1[0]), bf(peer_keys2[0]))
    out = _peer(hn, bf(expert_u[0]), bf(expert_v[0]).T, cnt, a, rank, b, h)
    return out.reshape(B, SEQ, D_MODEL)
```

```python
import functools
import math

import numpy as np
import jax
import jax.numpy as jnp
from jax import lax
from jax.experimental import pallas as pl
from jax.experimental.pallas import tpu as pltpu

F32 = jnp.float32
BF16 = jnp.bfloat16

D_MODEL = 1024
SEQ = 2048
MLA_HEADS = 8
QK_NOPE_DIM = 64
QK_ROPE_DIM = 32
QK_HEAD_DIM = QK_NOPE_DIM + QK_ROPE_DIM
V_HEAD_DIM = 64
Q_LORA_RANK = 256
KV_LORA_RANK = 128
ROPE_THETA = 10000.0
HYENA_WIDTH = 512
FILTER_EMB = 33
FILTER_ORDER = 64
FAST_DECAY_PCT = 0.3
SLOW_DECAY_PCT = 1.5
DECAY_TARGET = 1e-2
PEER_HEADS = 8
N_KEYS = 128
N_EXPERTS = N_KEYS * N_KEYS
PEER_KEY_DIM = 256
PEER_TOPK = 16
EPS = 1e-6

LANES = 128
HEAD_SLOT = LANES
FFT_N = 2 * SEQ
PROJ_COLS = 4096
COL_CQ, COL_CKV, COL_KPE, COL_HY, COL_GATE = 0, 256, 384, 512, 2048
KPE_LANE = QK_NOPE_DIM
VMEM_LIMIT = 56 * 1024 * 1024


def _cparams(sem):
    return pltpu.CompilerParams(dimension_semantics=sem, vmem_limit_bytes=VMEM_LIMIT)


def _dot(a, b):
    return jnp.dot(a, b, preferred_element_type=F32)


def _dot_nt(a, b):
    return lax.dot_general(a, b, (((1,), (1,)), ((), ())), preferred_element_type=F32)


def _rms(x, g):
    return x * lax.rsqrt(jnp.mean(x * x, axis=-1, keepdims=True) + EPS) * g


@functools.lru_cache(maxsize=None)
def _dft_tables():
    p = np.arange(FFT_N, dtype=np.int64)[:, None]
    s = np.arange(SEQ, dtype=np.int64)[None, :]
    f = np.where(p <= SEQ, p, p - SEQ)
    ang = (2.0 * np.pi / FFT_N) * ((f * s) % FFT_N).astype(np.float64)
    w = np.where(p <= SEQ, np.cos(ang), np.sin(ang)).astype(np.float32)
    return w, np.ascontiguousarray(w.T)


@functools.lru_cache(maxsize=None)
def _rotary_tables():
    half = QK_ROPE_DIM // 2
    inv_freq = ROPE_THETA ** (-np.arange(half, dtype=np.float64) / half)
    ang = np.arange(SEQ, dtype=np.float64)[:, None] * inv_freq[None, :]
    cos = np.ones((SEQ, HEAD_SLOT)); sa = np.zeros((SEQ, HEAD_SLOT)); sb = np.zeros((SEQ, HEAD_SLOT))
    lo, mid, hi = QK_NOPE_DIM, QK_NOPE_DIM + half, QK_HEAD_DIM
    cos[:, lo:mid] = np.cos(ang); cos[:, mid:hi] = np.cos(ang)
    sb[:, lo:mid] = -np.sin(ang)
    sa[:, mid:hi] = np.sin(ang)
    return tuple(jnp.asarray(t, dtype=F32) for t in (cos, sa, sb))


@functools.lru_cache(maxsize=None)
def _filter_consts():
    L = SEQ
    t = np.linspace(0.0, 1.0, L)[:, None]
    bands = (FILTER_EMB - 1) // 2
    w = 2.0 * math.pi * np.arange(L)[:, None] / L
    f = np.linspace(1e-4, bands - 1, bands)[None, :]
    z = np.concatenate([t, np.cos(f * w), -np.sin(f * w)], axis=-1)
    zp = np.zeros((L, LANES)); zp[:, :FILTER_EMB] = z
    min_decay = math.log(DECAY_TARGET) / SLOW_DECAY_PCT
    max_decay = math.log(DECAY_TARGET) / FAST_DECAY_PCT
    deltas = np.abs(np.linspace(min_decay, max_decay, HYENA_WIDTH))[None, :]
    return jnp.asarray(zp, dtype=F32), jnp.asarray(deltas, dtype=F32)


INPROJ_TM = 512
INPROJ_CW = 1024


def _inproj_kernel(x_ref, g_ref, w_ref, o_ref):
    xn = _rms(x_ref[...], g_ref[...]).astype(BF16)
    for c in range(PROJ_COLS // INPROJ_CW):
        sl = slice(c * INPROJ_CW, (c + 1) * INPROJ_CW)
        o_ref[:, sl] = _dot(xn, w_ref[:, sl]).astype(BF16)


def _inproj(x2, g, w_p):
    T = x2.shape[0]
    return pl.pallas_call(
        _inproj_kernel,
        grid=(T // INPROJ_TM,),
        in_specs=[pl.BlockSpec((INPROJ_TM, D_MODEL), lambda i: (i, 0)),
                  pl.BlockSpec((1, D_MODEL), lambda i: (0, 0)),
                  pl.BlockSpec((D_MODEL, PROJ_COLS), lambda i: (0, 0))],
        out_specs=pl.BlockSpec((INPROJ_TM, PROJ_COLS), lambda i: (i, 0)),
        out_shape=jax.ShapeDtypeStruct((T, PROJ_COLS), BF16),
        compiler_params=_cparams(("parallel",)),
        name="inproj",
    )(x2, g, w_p)


MLA_TM = 512
HW = MLA_HEADS * HEAD_SLOT


def _rope_partner(a, axis):
    half = QK_ROPE_DIM // 2
    lo, mid, hi = QK_NOPE_DIM, QK_NOPE_DIM + half, QK_HEAD_DIM
    take = lambda s, e: lax.slice_in_dim(a, s, e, axis=axis)
    pad = [(0, 0)] * a.ndim
    pad[axis] = (lo, a.shape[axis] - hi)
    return jnp.pad(jnp.concatenate([take(mid, hi), take(lo, mid)], axis=axis), pad)


def _mlaprep_kernel(p_ref, qa_ref, kva_ref, wuq_ref, wuqr_ref, wuk_ref, wuvt_ref, gq_ref, gqr_ref, gk_ref,
                    ones_ref, cos_ref, sa_ref, sb_ref, q_ref, k_ref, vt_ref):
    p = p_ref[...].astype(F32)
    cqn = _rms(p[:, COL_CQ:COL_CQ + Q_LORA_RANK], qa_ref[...]).astype(BF16)
    ckvn = _rms(p[:, COL_CKV:COL_CKV + KV_LORA_RANK], kva_ref[...]).astype(BF16)
    kpe = p[:, COL_KPE:COL_KPE + HEAD_SLOT]
    q = _dot(cqn, wuq_ref[...])
    qr = _dot(cqn, wuqr_ref[...])
    kn = _dot(ckvn, wuk_ref[...])
    lane = lax.broadcasted_iota(jnp.int32, ckvn.shape, 1)
    ckv_one = jnp.concatenate([ckvn, jnp.where(lane == 0, 1.0, 0.0).astype(BF16)], axis=1)
    vt_ref[...] = _dot_nt(wuvt_ref[...], ckv_one).astype(BF16)
    ones = ones_ref[...]
    pair = 2 * HEAD_SLOT

    def head_sums(x):
        sq = (x * x).astype(BF16)
        return jnp.concatenate([_dot(sq[:, i:i + pair], ones) for i in range(0, HW, pair)], axis=1)
    ssq = head_sums(q)
    ssk = head_sums(kn)
    sspe = _dot((kpe * kpe).astype(BF16), ones_ref[:HEAD_SLOT, :HEAD_SLOT])
    cos, sa, sb = cos_ref[...], sa_ref[...], sb_ref[...]
    scale = QK_HEAD_DIM ** -0.5 * math.log2(math.e)
    inv_d = 1.0 / QK_HEAD_DIM
    tq = cos * (gq_ref[...] * scale)
    tqr = (sa + sb) * (gqr_ref[...] * scale)
    tk = cos * gk_ref[...]
    kg = kpe * gk_ref[...]
    half = QK_ROPE_DIM // 2
    rk = pltpu.roll(kg, half, 1) * sa + pltpu.roll(kg, HEAD_SLOT - half, 1) * sb
    for h in range(MLA_HEADS):
        sl = slice(h * HEAD_SLOT, (h + 1) * HEAD_SLOT)
        inv_q = lax.rsqrt(ssq[:, sl] * inv_d + EPS)
        q_ref[:, sl] = ((q[:, sl] * tq + qr[:, sl] * tqr) * inv_q).astype(BF16)
        inv_k = lax.rsqrt((ssk[:, sl] + sspe) * inv_d + EPS)
        k_ref[:, sl] = (((kn[:, sl] + kpe) * tk + rk) * inv_k).astype(BF16)


def _mlaprep(proj, qa, kva, wuq_p, wuqr_p, wuk_p, wuvt_p, gq, gqr, gk):
    T = proj.shape[0]
    cos, sa, sb = _rotary_tables()
    head_of = np.arange(2 * HEAD_SLOT) // HEAD_SLOT
    ones = jnp.asarray(head_of[:, None] == head_of[None, :], dtype=BF16)
    nseq = SEQ // MLA_TM
    full = lambda shape: pl.BlockSpec(shape, lambda i: (0, 0))
    pos = pl.BlockSpec((MLA_TM, HEAD_SLOT), lambda i: (i % nseq, 0))
    out = pl.BlockSpec((MLA_TM, HW), lambda i: (i, 0))
    return pl.pallas_call(
        _mlaprep_kernel,
        grid=(T // MLA_TM,),
        in_specs=[pl.BlockSpec((MLA_TM, 512), lambda i: (i, 0)),
                  full((1, Q_LORA_RANK)), full((1, KV_LORA_RANK)),
                  full((Q_LORA_RANK, HW)), full((Q_LORA_RANK, HW)), full((KV_LORA_RANK, HW)),
                  full((HW, 2 * KV_LORA_RANK)),
                  full((1, HEAD_SLOT)), full((1, HEAD_SLOT)), full((1, HEAD_SLOT)),
                  full((2 * HEAD_SLOT, 2 * HEAD_SLOT)), pos, pos, pos],
        out_specs=[out, out, pl.BlockSpec((HW, MLA_TM), lambda i: (0, i))],
        out_shape=[jax.ShapeDtypeStruct((T, HW), BF16)] * 2 + [jax.ShapeDtypeStruct((HW, T), BF16)],
        compiler_params=_cparams(("parallel",)),
        name="mlaprep",
    )(proj, qa, kva, wuq_p, wuqr_p, wuk_p, wuvt_p, gq, gqr, gk, ones, cos, sa, sb)


ATT_TQ = 1024


def _attn_kernel(q_ref, k_ref, vt_ref, o_ref):
    outs = []
    for hh in range(2):
        sl = slice(hh * HEAD_SLOT, (hh + 1) * HEAD_SLOT)
        st = _dot_nt(k_ref[0, :, sl], q_ref[0, :, sl])
        m = jnp.max(st, axis=0, keepdims=True)
        p = jnp.exp2(st - m).astype(BF16)
        ot = _dot(vt_ref[sl, :], p)
        on = ot[:V_HEAD_DIM, :] * (1.0 / ot[V_HEAD_DIM:V_HEAD_DIM + 1, :])
        outs.append(on.T)
    o_ref[0] = jnp.concatenate(outs, axis=-1).astype(BF16)


def _attention(q, k, vt, B):
    q3, k3 = (a.reshape(B, SEQ, HW) for a in (q, k))
    pair = 2 * HEAD_SLOT
    return pl.pallas_call(
        _attn_kernel,
        grid=(B, MLA_HEADS // 2, SEQ // ATT_TQ),
        in_specs=[pl.BlockSpec((1, ATT_TQ, pair), lambda b, h, i: (b, i, h)),
                  pl.BlockSpec((1, SEQ, pair), lambda b, h, i: (b, 0, h)),
                  pl.BlockSpec((pair, SEQ), lambda b, h, i: (h, b))],
        out_specs=pl.BlockSpec((1, ATT_TQ, 2 * V_HEAD_DIM), lambda b, h, i: (b, i, h)),
        out_shape=jax.ShapeDtypeStruct((B, SEQ, MLA_HEADS * V_HEAD_DIM), BF16),
        compiler_params=_cparams(("parallel", "parallel", "parallel")),
        name="attn",
    )(q3, k3, vt)


FILT_TL = 512


def _filter_kernel(z_ref, w1_ref, b1_ref, w2_ref, b2_ref, w3_ref, b3_ref, w4_ref, b4_ref, fr_ref,
                   dl_ref, o_ref):
    hp = functools.partial(jnp.dot, precision=lax.Precision.HIGHEST, preferred_element_type=F32)
    z = z_ref[...]
    fr = fr_ref[...]
    h = jnp.sin(fr * (hp(z, w1_ref[...]) + b1_ref[...]))
    h = jnp.sin(fr * (hp(h, w2_ref[...]) + b2_ref[...]))
    h = jnp.sin(fr * (hp(h, w3_ref[...]) + b3_ref[...]))
    h4 = hp(h, w4_ref[...]) + b4_ref[...]
    decay = jnp.exp(-z[:, 0:1] * dl_ref[...])
    hf = h4[:, :HYENA_WIDTH] * decay
    hb = h4[:, HYENA_WIDTH:] * decay
    row = pl.program_id(0) * FILT_TL + lax.broadcasted_iota(jnp.int32, hb.shape, 0)
    hb = jnp.where(row == 0, 0.0, hb)
    o_ref[:, :HYENA_WIDTH] = (hf + hb).astype(BF16)
    o_ref[:, HYENA_WIDTH:] = (hf - hb).astype(BF16)


def _pad2(a, r, c):
    return jnp.zeros((r, c), a.dtype).at[:a.shape[0], :a.shape[1]].set(a)


def _filter(w1, b1, w2, b2, w3, b3, w4, b4, freq):
    z, deltas = _filter_consts()
    P = LANES
    args = (z, _pad2(w1, P, P), _pad2(b1[None], 1, P), _pad2(w2, P, P), _pad2(b2[None], 1, P),
            _pad2(w3, P, P), _pad2(b3[None], 1, P), _pad2(w4, P, 2 * HYENA_WIDTH), b4[None],
            _pad2(freq[None], 1, P), deltas)
    full = lambda a: pl.BlockSpec(a.shape, lambda i: (0, 0))
    return pl.pallas_call(
        _filter_kernel,
        grid=(SEQ // FILT_TL,),
        in_specs=[pl.BlockSpec((FILT_TL, P), lambda i: (i, 0))] + [full(a) for a in args[1:]],
        out_specs=pl.BlockSpec((FILT_TL, 2 * HYENA_WIDTH), lambda i: (i, 0)),
        out_shape=jax.ShapeDtypeStruct((SEQ, 2 * HYENA_WIDTH), BF16),
        compiler_params=_cparams(("parallel",)),
        name="filt",
    )(*args)


SPEC_TR = 512


def _kspec_kernel(w_ref, h_ref, o_ref):
    r = _dot(w_ref[...], h_ref[...])
    row = pl.program_id(0) * SPEC_TR + lax.broadcasted_iota(jnp.int32, (SPEC_TR, HYENA_WIDTH), 0)
    o_ref[...] = jnp.where(row <= SEQ, r[:, :HYENA_WIDTH], r[:, HYENA_WIDTH:])


def _kspec(wf, hsd):
    return pl.pallas_call(
        _kspec_kernel,
        grid=(FFT_N // SPEC_TR,),
        in_specs=[pl.BlockSpec((SPEC_TR, SEQ), lambda i: (i, 0)),
                  pl.BlockSpec((SEQ, 2 * HYENA_WIDTH), lambda i: (0, 0))],
        out_specs=pl.BlockSpec((SPEC_TR, HYENA_WIDTH), lambda i: (i, 0)),
        out_shape=jax.ShapeDtypeStruct((FFT_N, HYENA_WIDTH), F32),
        compiler_params=_cparams(("parallel",)),
        name="kspec",
    )(wf, hsd)


def _short_conv(u, w, b):
    row = lax.broadcasted_iota(jnp.int32, u.shape, 0)
    prev = jnp.where(row == 0, 0.0, pltpu.roll(u, 1, 0))
    nxt = jnp.where(row == SEQ - 1, 0.0, pltpu.roll(u, SEQ - 1, 0))
    return prev * w[0:1] + u * w[1:2] + nxt * w[2:3] + b


def _hyfwd_kernel(x0_ref, x1_ref, v_ref, cw_ref, cb_ref, bias_ref, wre_ref, wim_ref, kre_ref, kim_ref,
                  y_ref, p1_ref, p2_ref, z_ref):
    r = pl.program_id(1)
    C = HYENA_WIDTH

    @pl.when(r == 0)
    def _():
        cw = cw_ref[...]
        cb = cb_ref[...]
        x0 = _short_conv(x0_ref[0].astype(F32), cw[:, 0:C], cb[:, 0:C])
        x1 = _short_conv(x1_ref[0].astype(F32), cw[:, C:2 * C], cb[:, C:2 * C])
        v = _short_conv(v_ref[0].astype(F32), cw[:, 2 * C:], cb[:, 2 * C:])
        z = v * x1
        z_ref[...] = z.astype(BF16)
        p1_ref[0] = x0.astype(BF16)
        p2_ref[0] = (x0 * z * bias_ref[...]).astype(BF16)

    z = z_ref[...]
    a = _dot(wre_ref[...], z)
    q = _dot(wim_ref[...], z)
    ka = kre_ref[...]
    kq = kim_ref[...]
    first = jnp.logical_and(r == 0, lax.broadcasted_iota(jnp.int32, a.shape, 0) == 0)
    yr = jnp.where(first, a * ka, a * ka - q * kq)
    yq = jnp.where(first, q * kq, a * kq + q * ka)
    sc = jnp.where(first, 1.0 / FFT_N, 2.0 / FFT_N)
    y_ref[0, 0] = (yr * sc).astype(BF16)
    y_ref[0, 1] = (yq * sc).astype(BF16)


HY_TR = 512


def _hyfwd(proj3, cw, cb, bias, wf, kp):
    B = proj3.shape[0]
    nr = SEQ // HY_TR
    cblk = lambda c: pl.BlockSpec((1, SEQ, HYENA_WIDTH), lambda b, r: (b, 0, c))
    full = lambda a: pl.BlockSpec(a.shape, lambda b, r: (0, 0))
    seq_out = pl.BlockSpec((1, SEQ, HYENA_WIDTH), lambda b, r: (b, 0, 0))
    y, p1, p2 = pl.pallas_call(
        _hyfwd_kernel,
        grid=(B, nr),
        in_specs=[cblk(COL_HY // HYENA_WIDTH), cblk(COL_HY // HYENA_WIDTH + 1), cblk(COL_HY // HYENA_WIDTH + 2),
                  full(cw), full(cb), full(bias),
                  pl.BlockSpec((HY_TR, SEQ), lambda b, r: (r, 0)),
                  pl.BlockSpec((HY_TR, SEQ), lambda b, r: (r + nr, 0)),
                  pl.BlockSpec((HY_TR, HYENA_WIDTH), lambda b, r: (r, 0)),
                  pl.BlockSpec((HY_TR, HYENA_WIDTH), lambda b, r: (r + nr, 0))],
        out_specs=[pl.BlockSpec((1, 2, HY_TR, HYENA_WIDTH), lambda b, r: (b, 0, r, 0)), seq_out, seq_out],
        out_shape=[jax.ShapeDtypeStruct((B, 2, SEQ, HYENA_WIDTH), BF16),
                   jax.ShapeDtypeStruct((B, SEQ, HYENA_WIDTH), BF16),
                   jax.ShapeDtypeStruct((B, SEQ, HYENA_WIDTH), BF16)],
        scratch_shapes=[pltpu.VMEM((SEQ, HYENA_WIDTH), BF16)],
        compiler_params=_cparams(("parallel", "arbitrary")),
        name="hyfwd",
    )(proj3, proj3, proj3, cw, cb, bias, wf, wf, kp, kp)
    return y.reshape(B, FFT_N, HYENA_WIDTH), p1, p2


HYI_TT = 512


def _hyinv_kernel(wt_ref, y_ref, p1_ref, p2_ref, o_ref):
    conv = _dot(wt_ref[...], y_ref[0])
    o_ref[0] = (p1_ref[0].astype(F32) * conv + p2_ref[0].astype(F32)).astype(BF16)


def _hyinv(wft, ys, p1, p2):
    B = ys.shape[0]
    blk = pl.BlockSpec((1, HYI_TT, HYENA_WIDTH), lambda t, b: (b, t, 0))
    return pl.pallas_call(
        _hyinv_kernel,
        grid=(SEQ // HYI_TT, B),
        in_specs=[pl.BlockSpec((HYI_TT, FFT_N), lambda t, b: (t, 0)),
                  pl.BlockSpec((1, FFT_N, HYENA_WIDTH), lambda t, b: (b, 0, 0)),
                  blk, blk],
        out_specs=blk,
        out_shape=jax.ShapeDtypeStruct((B, SEQ, HYENA_WIDTH), BF16),
        compiler_params=_cparams(("parallel", "parallel")),
        name="hyinv",
    )(wft, ys, p1, p2)


MERGE_TM = 512
QP_SLOTS = PEER_HEADS * 2


def _merge_kernel(att_ref, hy_ref, gl_ref, bg_ref, x_ref, woa_ref, woh_ref, wout_ref, fg_ref, wq_ref,
                  h_ref, hn_ref, qp_ref):
    a = _dot(att_ref[...], woa_ref[...])
    yh = _dot(hy_ref[...], woh_ref[...])
    g = 1.0 / (1.0 + jnp.exp(-(gl_ref[...].astype(F32) + bg_ref[...])))
    merged = (g[:, :D_MODEL] * a + g[:, D_MODEL:] * yh).astype(BF16)
    h = x_ref[...] + _dot(merged, wout_ref[...])
    h_ref[...] = h
    hn = _rms(h, fg_ref[...])
    hn_ref[...] = (hn * INV_SQRT2).astype(BF16)
    qp = _dot(hn.astype(BF16), wq_ref[...])
    for c in range(QP_SLOTS):
        qp_ref[c] = qp[:, c * LANES:(c + 1) * LANES].astype(BF16)


def _merge(att, hy, proj, bg, x2, woa, woh, wout, fg, wq):
    T = x2.shape[0]
    half = PEER_KEY_DIM // 2
    row = lambda w: pl.BlockSpec((MERGE_TM, w), lambda i: (i, 0))
    full = lambda a: pl.BlockSpec(a.shape, lambda i: (0, 0))
    return pl.pallas_call(
        _merge_kernel,
        grid=(T // MERGE_TM,),
        in_specs=[row(att.shape[1]), row(hy.shape[1]),
                  pl.BlockSpec((MERGE_TM, 2 * D_MODEL), lambda i: (i, COL_GATE // (2 * D_MODEL))),
                  full(bg), row(D_MODEL), full(woa), full(woh), full(wout), full(fg), full(wq)],
        out_specs=[row(D_MODEL), row(D_MODEL),
                   pl.BlockSpec((QP_SLOTS, MERGE_TM, half), lambda i: (0, i, 0))],
        out_shape=[jax.ShapeDtypeStruct((T, D_MODEL), F32),
                   jax.ShapeDtypeStruct((T, D_MODEL), BF16),
                   jax.ShapeDtypeStruct((QP_SLOTS, T, half), BF16)],
        compiler_params=_cparams(("parallel",)),
        name="merge",
    )(att, hy, proj, bg, x2, woa, woh, wout, fg, wq)


ROUTE_TB = LANES
NEG_INF = float("-inf")
N_CHAINS = 2 * PEER_HEADS


def _sort_pairs(n):
    pairs = []

    def merge(lo, hi, r):
        step = 2 * r
        if step < hi - lo:
            merge(lo, hi, step)
            merge(lo + r, hi, step)
            pairs.extend((i, i + r) for i in range(lo + r, hi - r, step))
        else:
            pairs.append((lo, lo + r))

    def sort(lo, hi):
        if hi > lo:
            mid = lo + (hi - lo) // 2
            sort(lo, mid)
            sort(mid + 1, hi)
            merge(lo, hi, 1)
    sort(0, n - 1)
    return pairs


SUBLANES = 8


def _top_sorted(s):
    n = s.shape[0] // SUBLANES
    v = [s[SUBLANES * i:SUBLANES * (i + 1), :] for i in range(n)]
    for i, j in _sort_pairs(n):
        v[i], v[j] = jnp.maximum(v[i], v[j]), jnp.minimum(v[i], v[j])
    out = []
    for k in range(PEER_TOPK):
        m = jnp.max(v[0], axis=0, keepdims=True)
        out.append(m)
        if k + 1 < PEER_TOPK:
            popped = v[0] == m
            for d in range(PEER_TOPK - 1 - k):
                v[d] = jnp.where(popped, v[d + 1], v[d])
    return jnp.concatenate(out, axis=0)


CAND_ROWS = 72


def _extract_top(work_ref, vals_ref, n_chains):
    slot = lax.broadcasted_iota(jnp.int32, vals_ref.shape[1:], 0)

    def body(k, carry):
        for c in range(n_chains):
            s = work_ref[c]
            m = jnp.max(s, axis=0, keepdims=True)
            work_ref[c] = jnp.where(s == m, NEG_INF, s)
            vals_ref[c] = jnp.where(slot == k, m, vals_ref[c])
        return carry
    lax.fori_loop(0, PEER_TOPK, body, 0)


def _candidates(v1, v2):
    a16 = lax.broadcasted_iota(jnp.int32, v1.shape, 0)
    a8 = a16[0:8]
    return jnp.concatenate([
        v1[0:1] + v2,
        v1[1:2] + v2[0:8], v1[2:3] + v2[0:8], v1[3:4] + v2[0:8],
        jnp.where(a16 >= 4, v1 + v2[0:1], NEG_INF),
        jnp.where(a8 >= 4, v1[0:8] + v2[1:2], NEG_INF),
        jnp.where(a8 == 4, v1[0:8] + v2[2:3], NEG_INF),
    ], axis=0)


def _route_kernel(q_ref, k1_ref, k2_ref, cnt_ref, a_ref, rank_ref, b_ref,
                  s_ref, vals_ref, cand_ref, cvals_ref):
    for h in range(PEER_HEADS):
        for side, k_ref in enumerate((k1_ref, k2_ref)):
            s = _dot_nt(k_ref[h], q_ref[2 * h + side])
            s_ref[2 * h + side] = s
            vals_ref[2 * h + side] = _top_sorted(s)
    for h in range(PEER_HEADS):
        cand_ref[h] = _candidates(vals_ref[2 * h], vals_ref[2 * h + 1])
    cvals_ref[...] = jnp.zeros_like(cvals_ref)
    _extract_top(cand_ref, cvals_ref, PEER_HEADS)

    for h in range(PEER_HEADS):
        v1, v2, cv = vals_ref[2 * h], vals_ref[2 * h + 1], cvals_ref[h]
        s1, s2 = s_ref[2 * h], s_ref[2 * h + 1]
        tau = cv[PEER_TOPK - 1:PEER_TOPK]
        zsum = jnp.sum(jnp.exp(cv - cv[0:1]), axis=0, keepdims=True)
        cnt_sorted = jnp.zeros_like(v1)
        for b in range(PEER_TOPK):
            cnt_sorted = cnt_sorted + jnp.where(v1 + v2[b:b + 1] >= tau, 1.0, 0.0)
        cnt = jnp.zeros_like(s1)
        for a in range(PEER_TOPK):
            cnt = jnp.where(s1 == v1[a:a + 1], cnt_sorted[a:a + 1], cnt)
        rank = jnp.full_like(s2, float(PEER_TOPK))
        for b in reversed(range(PEER_TOPK)):
            rank = jnp.where(s2 >= v2[b:b + 1], float(b), rank)
        cnt_ref[0, h] = cnt * RANK_SCALE
        a_ref[0, h] = jnp.exp(s1 - v1[0:1]) * (GELU_SCALE / zsum)
        rank_ref[0, h] = (rank * RANK_SCALE).astype(BF16)
        b_ref[0, h] = jnp.exp(s2 - v2[0:1]).astype(BF16)


def _route(qp, k1, k2):
    T = qp.shape[1]
    half = PEER_KEY_DIM // 2
    blk = pl.BlockSpec((1, PEER_HEADS, N_KEYS, LANES), lambda i: (i, 0, 0, 0))
    shape = lambda dt: jax.ShapeDtypeStruct((T // LANES, PEER_HEADS, N_KEYS, LANES), dt)
    keys = pl.BlockSpec((PEER_HEADS, N_KEYS, half), lambda i: (0, 0, 0))
    return pl.pallas_call(
        _route_kernel,
        grid=(T // ROUTE_TB,),
        in_specs=[pl.BlockSpec((QP_SLOTS, ROUTE_TB, half), lambda i: (0, i, 0)), keys, keys],
        out_specs=[blk, blk, blk, blk],
        out_shape=[shape(F32), shape(F32), shape(BF16), shape(BF16)],
        scratch_shapes=[pltpu.VMEM((N_CHAINS, N_KEYS, ROUTE_TB), F32),
                        pltpu.VMEM((N_CHAINS, PEER_TOPK, ROUTE_TB), F32),
                        pltpu.VMEM((PEER_HEADS, CAND_ROWS, ROUTE_TB), F32),
                        pltpu.VMEM((PEER_HEADS, PEER_TOPK, ROUTE_TB), F32)],
        compiler_params=_cparams(("parallel",)),
        name="route",
    )(qp, k1, k2)


PEER_TB = 512
PEER_TE = 2048
PEER_ROWS = PEER_TE // N_KEYS
PEER_NC = PEER_TB // LANES
INV_SQRT2 = 1.0 / math.sqrt(2.0)
GELU_SCALE = 0.5 * math.sqrt(2.0)
RANK_SCALE = 256.0


def _row_tile(ref, c, h, ii):
    return jnp.broadcast_to(ref[c, h, ii:ii + 1, :], (N_KEYS, LANES)).astype(BF16)


def _peer_kernel(hn_ref, u_ref, vt_ref, cnt_ref, a_ref, rank_ref, b_ref, h_ref, o_ref, y_ref, act_ref, wt_ref):
    e = pl.program_id(1)

    @pl.when(e == 0)
    def _():
        y_ref[...] = jnp.zeros_like(y_ref)

    act = _dot_nt(u_ref[...], hn_ref[...])
    for c in range(PEER_NC):
        act_ref[c] = act[:, c * LANES:(c + 1) * LANES].astype(BF16)

    def chunk(c, carry):
        for ii in range(PEER_ROWS):
            rs = slice(ii * N_KEYS, (ii + 1) * N_KEYS)
            w = jnp.zeros((N_KEYS, LANES), BF16)
            for h in range(PEER_HEADS):
                cnt = _row_tile(cnt_ref, c, h, ii)
                wa = _row_tile(a_ref, c, h, ii)
                sel = jnp.minimum(jnp.maximum(cnt - rank_ref[c, h], 0), b_ref[c, h])
                w = w + sel * wa
            x = act_ref[c, rs, :]
            wt_ref[c, rs, :] = x * (1 + lax.erf(x)) * w
        return carry
    lax.fori_loop(0, PEER_NC, chunk, 0)

    wt = jnp.concatenate([wt_ref[c] for c in range(PEER_NC)], axis=1)
    y_ref[...] += _dot(vt_ref[...], wt)

    @pl.when(e == pl.num_programs(1) - 1)
    def _():
        o_ref[...] = h_ref[...] + y_ref[...].T


def _peer(hn, u, vt, cnt, a, rank, b, h):
    T = hn.shape[0]
    rows = pl.BlockSpec((PEER_NC, PEER_HEADS, PEER_ROWS, LANES), lambda t, e: (t, 0, e, 0))
    keys = pl.BlockSpec((PEER_NC, PEER_HEADS, N_KEYS, LANES), lambda t, e: (t, 0, 0, 0))
    tok = pl.BlockSpec((PEER_TB, D_MODEL), lambda t, e: (t, 0))
    return pl.pallas_call(
        _peer_kernel,
        grid=(T // PEER_TB, N_EXPERTS // PEER_TE),
        in_specs=[tok,
                  pl.BlockSpec((PEER_TE, D_MODEL), lambda t, e: (e, 0)),
                  pl.BlockSpec((D_MODEL, PEER_TE), lambda t, e: (0, e)),
                  rows, rows, keys, keys, tok],
        out_specs=tok,
        out_shape=jax.ShapeDtypeStruct((T, D_MODEL), F32),
        scratch_shapes=[pltpu.VMEM((D_MODEL, PEER_TB), F32),
                        pltpu.VMEM((PEER_NC, PEER_TE, LANES), BF16),
                        pltpu.VMEM((PEER_NC, PEER_TE, LANES), BF16)],
        compiler_params=_cparams(("parallel", "arbitrary")),
        name="peer",
    )(hn, u, vt, cnt, a, rank, b, h)


def _regroup_w_in(w):
    o_kv = Q_LORA_RANK
    o_pe = o_kv + KV_LORA_RANK
    o_hy = o_pe + QK_ROPE_DIM
    o_gate = o_hy + 3 * HYENA_WIDTH
    out = jnp.zeros((D_MODEL, PROJ_COLS), BF16)
    out = out.at[:, COL_CQ:COL_CQ + Q_LORA_RANK].set(w[:, :o_kv].astype(BF16))
    out = out.at[:, COL_CKV:COL_CKV + KV_LORA_RANK].set(w[:, o_kv:o_pe].astype(BF16))
    out = out.at[:, COL_KPE + KPE_LANE:COL_KPE + KPE_LANE + QK_ROPE_DIM].set(w[:, o_pe:o_hy].astype(BF16))
    out = out.at[:, COL_HY:COL_HY + 3 * HYENA_WIDTH].set(w[:, o_hy:o_gate].astype(BF16))
    out = out.at[:, COL_GATE:].set(w[:, o_gate:].astype(BF16))
    return out


def _head_slots(w, width):
    k = w.shape[0]
    w3 = w.reshape(k, MLA_HEADS, width).astype(BF16)
    return jnp.zeros((k, MLA_HEADS, HEAD_SLOT), BF16).at[:, :, :width].set(w3).reshape(k, HW)


def kernel(x, attn_norm, w_in, b_gate, q_a_norm, w_uq, kv_a_norm, w_ukv, q_norm, k_norm, w_o_attn, hyena_conv_w, hyena_conv_b, filt_w1, filt_b1, filt_w2, filt_b2, filt_w3, filt_b3, filt_w4, filt_b4, filt_freq, hyena_bias, w_o_hyena, w_out, ffn_norm, peer_w_q, peer_keys1, peer_keys2, expert_u, expert_v):
    B = x.shape[0]
    T = B * SEQ
    x2 = x.reshape(T, D_MODEL)
    bf = lambda a: a.astype(BF16)

    proj = _inproj(x2, attn_norm, _regroup_w_in(w_in[0]))
    kv3 = w_ukv[0].reshape(KV_LORA_RANK, MLA_HEADS, QK_NOPE_DIM + V_HEAD_DIM)
    wuk_p = _head_slots(kv3[:, :, :QK_NOPE_DIM].reshape(KV_LORA_RANK, -1), QK_NOPE_DIM)
    wuv_p = _head_slots(kv3[:, :, QK_NOPE_DIM:].reshape(KV_LORA_RANK, -1), V_HEAD_DIM)
    vone = jnp.zeros((MLA_HEADS, HEAD_SLOT), BF16).at[:, V_HEAD_DIM].set(1.0).reshape(HW, 1)
    wuvt_p = jnp.concatenate([wuv_p.T, _pad2(vone, HW, KV_LORA_RANK)], axis=1)
    wuq_p = _head_slots(w_uq[0], QK_HEAD_DIM)
    wuqr_p = _rope_partner(wuq_p.reshape(Q_LORA_RANK, MLA_HEADS, HEAD_SLOT), axis=2).reshape(Q_LORA_RANK, HW)
    gq = _pad2(q_norm, 1, HEAD_SLOT)
    q, k, vt = _mlaprep(proj, q_a_norm, kv_a_norm, wuq_p, wuqr_p, wuk_p, wuvt_p,
                        gq, _rope_partner(gq, axis=1), _pad2(k_norm, 1, HEAD_SLOT))
    att = _attention(q, k, vt, B).reshape(T, MLA_HEADS * V_HEAD_DIM)

    wf, wft = (jnp.asarray(w).astype(BF16) for w in _dft_tables())
    hsd = _filter(filt_w1[0], filt_b1[0], filt_w2[0], filt_b2[0], filt_w3[0], filt_b3[0], filt_w4[0],
                  filt_b4[0], filt_freq[0])
    kp = _kspec(wf, hsd)
    ys, p1, p2 = _hyfwd(proj.reshape(B, SEQ, PROJ_COLS), hyena_conv_w[0], hyena_conv_b, hyena_bias, wf, kp)
    hy = _hyinv(wft, ys, p1, p2).reshape(T, HYENA_WIDTH)

    h, hn, qp = _merge(att, hy, proj, b_gate, x2, bf(w_o_attn[0]), bf(w_o_hyena[0]), bf(w_out[0]),
                       ffn_norm, bf(peer_w_q[0]))

    cnt, a, rank, b = _route(qp, bf(peer_keys1[0]), bf(peer_keys2[0]))
    out = _peer(hn, bf(expert_u[0]), bf(expert_v[0]).T, cnt, a, rank, b, h)
    return out.reshape(B, SEQ, D_MODEL)
```

```python
import functools
import math

import numpy as np
import jax
import jax.numpy as jnp
from jax import lax
from jax.experimental import pallas as pl
from jax.experimental.pallas import tpu as pltpu

F32 = jnp.float32
BF16 = jnp.bfloat16

D_MODEL = 1024
SEQ = 2048
MLA_HEADS = 8
QK_NOPE_DIM = 64
QK_ROPE_DIM = 32
QK_HEAD_DIM = QK_NOPE_DIM + QK_ROPE_DIM
V_HEAD_DIM = 64
Q_LORA_RANK = 256
KV_LORA_RANK = 128
ROPE_THETA = 10000.0
HYENA_WIDTH = 512
FILTER_EMB = 33
FILTER_ORDER = 64
FAST_DECAY_PCT = 0.3
SLOW_DECAY_PCT = 1.5
DECAY_TARGET = 1e-2
PEER_HEADS = 8
N_KEYS = 128
N_EXPERTS = N_KEYS * N_KEYS
PEER_KEY_DIM = 256
PEER_TOPK = 16
EPS = 1e-6

LANES = 128
HEAD_SLOT = LANES
FFT_N = 2 * SEQ
PROJ_COLS = 4096
COL_CQ, COL_CKV, COL_KPE, COL_HY, COL_GATE = 0, 256, 384, 512, 2048
KPE_LANE = QK_NOPE_DIM
VMEM_LIMIT = 56 * 1024 * 1024


def _cparams(sem):
    return pltpu.CompilerParams(dimension_semantics=sem, vmem_limit_bytes=VMEM_LIMIT)


def _dot(a, b):
    return jnp.dot(a, b, preferred_element_type=F32)


def _dot_nt(a, b):
    return lax.dot_general(a, b, (((1,), (1,)), ((), ())), preferred_element_type=F32)


def _rms(x, g):
    return x * lax.rsqrt(jnp.mean(x * x, axis=-1, keepdims=True) + EPS) * g


@functools.lru_cache(maxsize=None)
def _dft_tables():
    p = np.arange(FFT_N, dtype=np.int64)[:, None]
    s = np.arange(SEQ, dtype=np.int64)[None, :]
    f = np.where(p <= SEQ, p, p - SEQ)
    ang = (2.0 * np.pi / FFT_N) * ((f * s) % FFT_N).astype(np.float64)
    w = np.where(p <= SEQ, np.cos(ang), np.sin(ang)).astype(np.float32)
    return w, np.ascontiguousarray(w.T)


@functools.lru_cache(maxsize=None)
def _rotary_tables():
    half = QK_ROPE_DIM // 2
    inv_freq = ROPE_THETA ** (-np.arange(half, dtype=np.float64) / half)
    ang = np.arange(SEQ, dtype=np.float64)[:, None] * inv_freq[None, :]
    cos = np.ones((SEQ, HEAD_SLOT)); sa = np.zeros((SEQ, HEAD_SLOT)); sb = np.zeros((SEQ, HEAD_SLOT))
    lo, mid, hi = QK_NOPE_DIM, QK_NOPE_DIM + half, QK_HEAD_DIM
    cos[:, lo:mid] = np.cos(ang); cos[:, mid:hi] = np.cos(ang)
    sb[:, lo:mid] = -np.sin(ang)
    sa[:, mid:hi] = np.sin(ang)
    return tuple(t.astype(np.float32) for t in (cos, sa, sb))


@functools.lru_cache(maxsize=None)
def _filter_consts():
    L = SEQ
    t = np.linspace(0.0, 1.0, L)[:, None]
    bands = (FILTER_EMB - 1) // 2
    w = 2.0 * math.pi * np.arange(L)[:, None] / L
    f = np.linspace(1e-4, bands - 1, bands)[None, :]
    z = np.concatenate([t, np.cos(f * w), -np.sin(f * w)], axis=-1)
    zp = np.zeros((L, LANES)); zp[:, :FILTER_EMB] = z
    min_decay = math.log(DECAY_TARGET) / SLOW_DECAY_PCT
    max_decay = math.log(DECAY_TARGET) / FAST_DECAY_PCT
    deltas = np.abs(np.linspace(min_decay, max_decay, HYENA_WIDTH))[None, :]
    return zp.astype(np.float32), deltas.astype(np.float32)


INPROJ_TM = 1024
INPROJ_CW = 1024


def _inproj_kernel(x_ref, g_ref, w_ref, o_ref):
    xn = _rms(x_ref[...], g_ref[...]).astype(BF16)
    for c in range(PROJ_COLS // INPROJ_CW):
        sl = slice(c * INPROJ_CW, (c + 1) * INPROJ_CW)
        o_ref[:, sl] = _dot(xn, w_ref[:, sl]).astype(BF16)


def _inproj(x2, g, w_p):
    T = x2.shape[0]
    return pl.pallas_call(
        _inproj_kernel,
        grid=(T // INPROJ_TM,),
        in_specs=[pl.BlockSpec((INPROJ_TM, D_MODEL), lambda i: (i, 0)),
                  pl.BlockSpec((1, D_MODEL), lambda i: (0, 0)),
                  pl.BlockSpec((D_MODEL, PROJ_COLS), lambda i: (0, 0))],
        out_specs=pl.BlockSpec((INPROJ_TM, PROJ_COLS), lambda i: (i, 0)),
        out_shape=jax.ShapeDtypeStruct((T, PROJ_COLS), BF16),
        compiler_params=_cparams(("parallel",)),
        name="inproj",
    )(x2, g, w_p)


MLA_TM = 1024
HW = MLA_HEADS * HEAD_SLOT


def _rope_partner(a, axis):
    half = QK_ROPE_DIM // 2
    lo, mid, hi = QK_NOPE_DIM, QK_NOPE_DIM + half, QK_HEAD_DIM
    take = lambda s, e: lax.slice_in_dim(a, s, e, axis=axis)
    pad = [(0, 0)] * a.ndim
    pad[axis] = (lo, a.shape[axis] - hi)
    return jnp.pad(jnp.concatenate([take(mid, hi), take(lo, mid)], axis=axis), pad)


def _mlaprep_kernel(p_ref, qa_ref, kva_ref, wuq_ref, wuqr_ref, wuk_ref, wuvt_ref, gq_ref, gqr_ref, gk_ref,
                    ones_ref, cos_ref, sa_ref, sb_ref, q_ref, k_ref, vt_ref):
    p = p_ref[...].astype(F32)
    cqn = _rms(p[:, COL_CQ:COL_CQ + Q_LORA_RANK], qa_ref[...]).astype(BF16)
    ckvn = _rms(p[:, COL_CKV:COL_CKV + KV_LORA_RANK], kva_ref[...]).astype(BF16)
    kpe = p[:, COL_KPE:COL_KPE + HEAD_SLOT]
    q = _dot(cqn, wuq_ref[...])
    qr = _dot(cqn, wuqr_ref[...])
    kn = _dot(ckvn, wuk_ref[...])
    lane = lax.broadcasted_iota(jnp.int32, ckvn.shape, 1)
    ckv_one = jnp.concatenate([ckvn, jnp.where(lane == 0, 1.0, 0.0).astype(BF16)], axis=1)
    vt_ref[...] = _dot_nt(wuvt_ref[...], ckv_one).astype(BF16)
    ones = ones_ref[...]
    pair = 2 * HEAD_SLOT

    def head_sums(x):
        sq = (x * x).astype(BF16)
        return jnp.concatenate([_dot(sq[:, i:i + pair], ones) for i in range(0, HW, pair)], axis=1)
    ssq = head_sums(q)
    ssk = head_sums(kn)
    sspe = _dot((kpe * kpe).astype(BF16), ones_ref[:HEAD_SLOT, :HEAD_SLOT])
    cos, sa, sb = cos_ref[...], sa_ref[...], sb_ref[...]
    scale = QK_HEAD_DIM ** -0.5 * math.log2(math.e)
    inv_d = 1.0 / QK_HEAD_DIM
    tq = cos * (gq_ref[...] * scale)
    tqr = (sa + sb) * (gqr_ref[...] * scale)
    tk = cos * gk_ref[...]
    kg = kpe * gk_ref[...]
    half = QK_ROPE_DIM // 2
    rk = pltpu.roll(kg, half, 1) * sa + pltpu.roll(kg, HEAD_SLOT - half, 1) * sb
    for h in range(MLA_HEADS):
        sl = slice(h * HEAD_SLOT, (h + 1) * HEAD_SLOT)
        inv_q = lax.rsqrt(ssq[:, sl] * inv_d + EPS)
        q_ref[:, sl] = ((q[:, sl] * tq + qr[:, sl] * tqr) * inv_q).astype(BF16)
        inv_k = lax.rsqrt((ssk[:, sl] + sspe) * inv_d + EPS)
        k_ref[:, sl] = (((kn[:, sl] + kpe) * tk + rk) * inv_k).astype(BF16)


def _mlaprep(proj, qa, kva, wuq_p, wuqr_p, wuk_p, wuvt_p, gq, gqr, gk):
    T = proj.shape[0]
    cos, sa, sb = (jnp.asarray(t) for t in _rotary_tables())
    head_of = np.arange(2 * HEAD_SLOT) // HEAD_SLOT
    ones = jnp.asarray(head_of[:, None] == head_of[None, :], dtype=BF16)
    nseq = SEQ // MLA_TM
    full = lambda shape: pl.BlockSpec(shape, lambda i: (0, 0))
    pos = pl.BlockSpec((MLA_TM, HEAD_SLOT), lambda i: (i % nseq, 0))
    out = pl.BlockSpec((MLA_TM, HW), lambda i: (i, 0))
    return pl.pallas_call(
        _mlaprep_kernel,
        grid=(T // MLA_TM,),
        in_specs=[pl.BlockSpec((MLA_TM, 512), lambda i: (i, 0)),
                  full((1, Q_LORA_RANK)), full((1, KV_LORA_RANK)),
                  full((Q_LORA_RANK, HW)), full((Q_LORA_RANK, HW)), full((KV_LORA_RANK, HW)),
                  full((HW, 2 * KV_LORA_RANK)),
                  full((1, HEAD_SLOT)), full((1, HEAD_SLOT)), full((1, HEAD_SLOT)),
                  full((2 * HEAD_SLOT, 2 * HEAD_SLOT)), pos, pos, pos],
        out_specs=[out, out, pl.BlockSpec((HW, MLA_TM), lambda i: (0, i))],
        out_shape=[jax.ShapeDtypeStruct((T, HW), BF16)] * 2 + [jax.ShapeDtypeStruct((HW, T), BF16)],
        compiler_params=_cparams(("parallel",)),
        name="mlaprep",
    )(proj, qa, kva, wuq_p, wuqr_p, wuk_p, wuvt_p, gq, gqr, gk, ones, cos, sa, sb)


ATT_TQ = SEQ


def _attn_kernel(q_ref, k_ref, vt_ref, o_ref):
    outs = []
    for hh in range(2):
        sl = slice(hh * HEAD_SLOT, (hh + 1) * HEAD_SLOT)
        st = _dot_nt(k_ref[0, :, sl], q_ref[0, :, sl])
        m = jnp.max(st, axis=0, keepdims=True)
        p = jnp.exp2(st - m).astype(BF16)
        ot = _dot(vt_ref[sl, :], p)
        on = ot[:V_HEAD_DIM, :] * (1.0 / ot[V_HEAD_DIM:V_HEAD_DIM + 1, :])
        outs.append(on.T)
    o_ref[0] = jnp.concatenate(outs, axis=-1).astype(BF16)


def _attention(q, k, vt, B):
    q3, k3 = (a.reshape(B, SEQ, HW) for a in (q, k))
    pair = 2 * HEAD_SLOT
    return pl.pallas_call(
        _attn_kernel,
        grid=(B, MLA_HEADS // 2, SEQ // ATT_TQ),
        in_specs=[pl.BlockSpec((1, ATT_TQ, pair), lambda b, h, i: (b, i, h)),
                  pl.BlockSpec((1, SEQ, pair), lambda b, h, i: (b, 0, h)),
                  pl.BlockSpec((pair, SEQ), lambda b, h, i: (h, b))],
        out_specs=pl.BlockSpec((1, ATT_TQ, 2 * V_HEAD_DIM), lambda b, h, i: (b, i, h)),
        out_shape=jax.ShapeDtypeStruct((B, SEQ, MLA_HEADS * V_HEAD_DIM), BF16),
        compiler_params=_cparams(("parallel", "parallel", "parallel")),
        name="attn",
    )(q3, k3, vt)


FILT_TL = 512


def _filter_kernel(z_ref, w1_ref, b1_ref, w2_ref, b2_ref, w3_ref, b3_ref, w4_ref, b4_ref, fr_ref,
                   dl_ref, o_ref):
    hp = functools.partial(jnp.dot, precision=lax.Precision.HIGHEST, preferred_element_type=F32)
    z = z_ref[...]
    fr = fr_ref[...]
    h = jnp.sin(fr * (hp(z, w1_ref[...]) + b1_ref[...]))
    h = jnp.sin(fr * (hp(h, w2_ref[...]) + b2_ref[...]))
    h = jnp.sin(fr * (hp(h, w3_ref[...]) + b3_ref[...]))
    h4 = hp(h, w4_ref[...]) + b4_ref[...]
    decay = jnp.exp(-z[:, 0:1] * dl_ref[...])
    hf = h4[:, :HYENA_WIDTH] * decay
    hb = h4[:, HYENA_WIDTH:] * decay
    row = pl.program_id(0) * FILT_TL + lax.broadcasted_iota(jnp.int32, hb.shape, 0)
    hb = jnp.where(row == 0, 0.0, hb)
    o_ref[:, :HYENA_WIDTH] = (hf + hb).astype(BF16)
    o_ref[:, HYENA_WIDTH:] = (hf - hb).astype(BF16)


def _pad2(a, r, c):
    return jnp.zeros((r, c), a.dtype).at[:a.shape[0], :a.shape[1]].set(a)


def _filter(w1, b1, w2, b2, w3, b3, w4, b4, freq):
    z, deltas = (jnp.asarray(t) for t in _filter_consts())
    P = LANES
    args = (z, _pad2(w1, P, P), _pad2(b1[None], 1, P), _pad2(w2, P, P), _pad2(b2[None], 1, P),
            _pad2(w3, P, P), _pad2(b3[None], 1, P), _pad2(w4, P, 2 * HYENA_WIDTH), b4[None],
            _pad2(freq[None], 1, P), deltas)
    full = lambda a: pl.BlockSpec(a.shape, lambda i: (0, 0))
    return pl.pallas_call(
        _filter_kernel,
        grid=(SEQ // FILT_TL,),
        in_specs=[pl.BlockSpec((FILT_TL, P), lambda i: (i, 0))] + [full(a) for a in args[1:]],
        out_specs=pl.BlockSpec((FILT_TL, 2 * HYENA_WIDTH), lambda i: (i, 0)),
        out_shape=jax.ShapeDtypeStruct((SEQ, 2 * HYENA_WIDTH), BF16),
        compiler_params=_cparams(("parallel",)),
        name="filt",
    )(*args)


SPEC_TR = 512


def _kspec_kernel(w_ref, h_ref, o_ref):
    r = _dot(w_ref[...], h_ref[...])
    row = pl.program_id(0) * SPEC_TR + lax.broadcasted_iota(jnp.int32, (SPEC_TR, HYENA_WIDTH), 0)
    o_ref[...] = jnp.where(row <= SEQ, r[:, :HYENA_WIDTH], r[:, HYENA_WIDTH:])


def _kspec(wf, hsd):
    return pl.pallas_call(
        _kspec_kernel,
        grid=(FFT_N // SPEC_TR,),
        in_specs=[pl.BlockSpec((SPEC_TR, SEQ), lambda i: (i, 0)),
                  pl.BlockSpec((SEQ, 2 * HYENA_WIDTH), lambda i: (0, 0))],
        out_specs=pl.BlockSpec((SPEC_TR, HYENA_WIDTH), lambda i: (i, 0)),
        out_shape=jax.ShapeDtypeStruct((FFT_N, HYENA_WIDTH), F32),
        compiler_params=_cparams(("parallel",)),
        name="kspec",
    )(wf, hsd)


def _short_conv(u, w, b):
    row = lax.broadcasted_iota(jnp.int32, u.shape, 0)
    prev = jnp.where(row == 0, 0.0, pltpu.roll(u, 1, 0))
    nxt = jnp.where(row == SEQ - 1, 0.0, pltpu.roll(u, SEQ - 1, 0))
    return prev * w[0:1] + u * w[1:2] + nxt * w[2:3] + b


def _hyfwd_kernel(x0_ref, x1_ref, v_ref, cw_ref, cb_ref, bias_ref, wre_ref, wim_ref, kre_ref, kim_ref,
                  y_ref, p1_ref, p2_ref, z_ref):
    r = pl.program_id(1)
    C = HYENA_WIDTH

    @pl.when(r == 0)
    def _():
        cw = cw_ref[...]
        cb = cb_ref[...]
        x0 = _short_conv(x0_ref[0].astype(F32), cw[:, 0:C], cb[:, 0:C])
        x1 = _short_conv(x1_ref[0].astype(F32), cw[:, C:2 * C], cb[:, C:2 * C])
        v = _short_conv(v_ref[0].astype(F32), cw[:, 2 * C:], cb[:, 2 * C:])
        z = v * x1
        z_ref[...] = z.astype(BF16)
        p1_ref[0] = x0.astype(BF16)
        p2_ref[0] = (x0 * z * bias_ref[...]).astype(BF16)

    z = z_ref[...]
    a = _dot(wre_ref[...], z)
    q = _dot(wim_ref[...], z)
    ka = kre_ref[...]
    kq = kim_ref[...]
    first = jnp.logical_and(r == 0, lax.broadcasted_iota(jnp.int32, a.shape, 0) == 0)
    yr = jnp.where(first, a * ka, a * ka - q * kq)
    yq = jnp.where(first, q * kq, a * kq + q * ka)
    sc = jnp.where(first, 1.0 / FFT_N, 2.0 / FFT_N)
    y_ref[0, 0] = (yr * sc).astype(BF16)
    y_ref[0, 1] = (yq * sc).astype(BF16)


HY_TR = 512


def _hyfwd(proj3, cw, cb, bias, wf, kp):
    B = proj3.shape[0]
    nr = SEQ // HY_TR
    cblk = lambda c: pl.BlockSpec((1, SEQ, HYENA_WIDTH), lambda b, r: (b, 0, c))
    full = lambda a: pl.BlockSpec(a.shape, lambda b, r: (0, 0))
    seq_out = pl.BlockSpec((1, SEQ, HYENA_WIDTH), lambda b, r: (b, 0, 0))
    y, p1, p2 = pl.pallas_call(
        _hyfwd_kernel,
        grid=(B, nr),
        in_specs=[cblk(COL_HY // HYENA_WIDTH), cblk(COL_HY // HYENA_WIDTH + 1), cblk(COL_HY // HYENA_WIDTH + 2),
                  full(cw), full(cb), full(bias),
                  pl.BlockSpec((HY_TR, SEQ), lambda b, r: (r, 0)),
                  pl.BlockSpec((HY_TR, SEQ), lambda b, r: (r + nr, 0)),
                  pl.BlockSpec((HY_TR, HYENA_WIDTH), lambda b, r: (r, 0)),
                  pl.BlockSpec((HY_TR, HYENA_WIDTH), lambda b, r: (r + nr, 0))],
        out_specs=[pl.BlockSpec((1, 2, HY_TR, HYENA_WIDTH), lambda b, r: (b, 0, r, 0)), seq_out, seq_out],
        out_shape=[jax.ShapeDtypeStruct((B, 2, SEQ, HYENA_WIDTH), BF16),
                   jax.ShapeDtypeStruct((B, SEQ, HYENA_WIDTH), BF16),
                   jax.ShapeDtypeStruct((B, SEQ, HYENA_WIDTH), BF16)],
        scratch_shapes=[pltpu.VMEM((SEQ, HYENA_WIDTH), BF16)],
        compiler_params=_cparams(("parallel", "arbitrary")),
        name="hyfwd",
    )(proj3, proj3, proj3, cw, cb, bias, wf, wf, kp, kp)
    return y.reshape(B, FFT_N, HYENA_WIDTH), p1, p2


HYI_TT = 1024


def _hyinv_kernel(wt_ref, y_ref, p1_ref, p2_ref, o_ref):
    conv = _dot(wt_ref[...], y_ref[0])
    o_ref[0] = (p1_ref[0].astype(F32) * conv + p2_ref[0].astype(F32)).astype(BF16)


def _hyinv(wft, ys, p1, p2):
    B = ys.shape[0]
    blk = pl.BlockSpec((1, HYI_TT, HYENA_WIDTH), lambda t, b: (b, t, 0))
    return pl.pallas_call(
        _hyinv_kernel,
        grid=(SEQ // HYI_TT, B),
        in_specs=[pl.BlockSpec((HYI_TT, FFT_N), lambda t, b: (t, 0)),
                  pl.BlockSpec((1, FFT_N, HYENA_WIDTH), lambda t, b: (b, 0, 0)),
                  blk, blk],
        out_specs=blk,
        out_shape=jax.ShapeDtypeStruct((B, SEQ, HYENA_WIDTH), BF16),
        compiler_params=_cparams(("parallel", "parallel")),
        name="hyinv",
    )(wft, ys, p1, p2)


MERGE_TM = 1024
QP_SLOTS = PEER_HEADS * 2


def _merge_kernel(att_ref, hy_ref, gl_ref, bg_ref, x_ref, woa_ref, woh_ref, wout_ref, fg_ref, wq_ref,
                  h_ref, hn_ref, qp_ref):
    a = _dot(att_ref[...], woa_ref[...])
    yh = _dot(hy_ref[...], woh_ref[...])
    g = 1.0 / (1.0 + jnp.exp(-(gl_ref[...].astype(F32) + bg_ref[...])))
    merged = (g[:, :D_MODEL] * a + g[:, D_MODEL:] * yh).astype(BF16)
    h = x_ref[...] + _dot(merged, wout_ref[...])
    h_ref[...] = h
    hn = _rms(h, fg_ref[...])
    hn_ref[...] = (hn * INV_SQRT2).astype(BF16)
    qp = _dot(hn.astype(BF16), wq_ref[...])
    for c in range(QP_SLOTS):
        qp_ref[c] = qp[:, c * LANES:(c + 1) * LANES].astype(BF16)


def _merge(att, hy, proj, bg, x2, woa, woh, wout, fg, wq):
    T = x2.shape[0]
    half = PEER_KEY_DIM // 2
    row = lambda w: pl.BlockSpec((MERGE_TM, w), lambda i: (i, 0))
    full = lambda a: pl.BlockSpec(a.shape, lambda i: (0, 0))
    return pl.pallas_call(
        _merge_kernel,
        grid=(T // MERGE_TM,),
        in_specs=[row(att.shape[1]), row(hy.shape[1]),
                  pl.BlockSpec((MERGE_TM, 2 * D_MODEL), lambda i: (i, COL_GATE // (2 * D_MODEL))),
                  full(bg), row(D_MODEL), full(woa), full(woh), full(wout), full(fg), full(wq)],
        out_specs=[row(D_MODEL), row(D_MODEL),
                   pl.BlockSpec((QP_SLOTS, MERGE_TM, half), lambda i: (0, i, 0))],
        out_shape=[jax.ShapeDtypeStruct((T, D_MODEL), F32),
                   jax.ShapeDtypeStruct((T, D_MODEL), BF16),
                   jax.ShapeDtypeStruct((QP_SLOTS, T, half), BF16)],
        compiler_params=_cparams(("parallel",)),
        name="merge",
    )(att, hy, proj, bg, x2, woa, woh, wout, fg, wq)


ROUTE_TB = LANES
NEG_INF = float("-inf")
N_CHAINS = 2 * PEER_HEADS


def _sort_pairs(n):
    pairs = []

    def merge(lo, hi, r):
        step = 2 * r
        if step < hi - lo:
            merge(lo, hi, step)
            merge(lo + r, hi, step)
            pairs.extend((i, i + r) for i in range(lo + r, hi - r, step))
        else:
            pairs.append((lo, lo + r))

    def sort(lo, hi):
        if hi > lo:
            mid = lo + (hi - lo) // 2
            sort(lo, mid)
            sort(mid + 1, hi)
            merge(lo, hi, 1)
    sort(0, n - 1)
    return pairs


SUBLANES = 8


def _top_sorted(s):
    n = s.shape[0] // SUBLANES
    v = [s[SUBLANES * i:SUBLANES * (i + 1), :] for i in range(n)]
    for i, j in _sort_pairs(n):
        v[i], v[j] = jnp.maximum(v[i], v[j]), jnp.minimum(v[i], v[j])
    out = []
    for k in range(PEER_TOPK):
        m = jnp.max(v[0], axis=0, keepdims=True)
        out.append(m)
        if k + 1 < PEER_TOPK:
            popped = v[0] == m
            for d in range(PEER_TOPK - 1 - k):
                v[d] = jnp.where(popped, v[d + 1], v[d])
    return jnp.concatenate(out, axis=0)


CAND_ROWS = 72


def _extract_top(work_ref, vals_ref, n_chains):
    slot = lax.broadcasted_iota(jnp.int32, vals_ref.shape[1:], 0)

    def body(k, carry):
        for c in range(n_chains):
            s = work_ref[c]
            m = jnp.max(s, axis=0, keepdims=True)
            work_ref[c] = jnp.where(s == m, NEG_INF, s)
            vals_ref[c] = jnp.where(slot == k, m, vals_ref[c])
        return carry
    lax.fori_loop(0, PEER_TOPK, body, 0)


def _candidates(v1, v2):
    a16 = lax.broadcasted_iota(jnp.int32, v1.shape, 0)
    a8 = a16[0:8]
    return jnp.concatenate([
        v1[0:1] + v2,
        v1[1:2] + v2[0:8], v1[2:3] + v2[0:8], v1[3:4] + v2[0:8],
        jnp.where(a16 >= 4, v1 + v2[0:1], NEG_INF),
        jnp.where(a8 >= 4, v1[0:8] + v2[1:2], NEG_INF),
        jnp.where(a8 == 4, v1[0:8] + v2[2:3], NEG_INF),
    ], axis=0)


def _route_kernel(q_ref, k1_ref, k2_ref, cnt_ref, a_ref, rank_ref, b_ref,
                  s_ref, vals_ref, cand_ref, cvals_ref):
    for h in range(PEER_HEADS):
        for side, k_ref in enumerate((k1_ref, k2_ref)):
            s = _dot_nt(k_ref[h], q_ref[2 * h + side])
            s_ref[2 * h + side] = s
            vals_ref[2 * h + side] = _top_sorted(s)
    for h in range(PEER_HEADS):
        cand_ref[h] = _candidates(vals_ref[2 * h], vals_ref[2 * h + 1])
    cvals_ref[...] = jnp.zeros_like(cvals_ref)
    _extract_top(cand_ref, cvals_ref, PEER_HEADS)

    for h in range(PEER_HEADS):
        v1, v2, cv = vals_ref[2 * h], vals_ref[2 * h + 1], cvals_ref[h]
        s1, s2 = s_ref[2 * h], s_ref[2 * h + 1]
        tau = cv[PEER_TOPK - 1:PEER_TOPK]
        zsum = jnp.sum(jnp.exp(cv - cv[0:1]), axis=0, keepdims=True)
        cnt_sorted = jnp.zeros_like(v1)
        for b in range(PEER_TOPK):
            cnt_sorted = cnt_sorted + jnp.where(v1 + v2[b:b + 1] >= tau, 1.0, 0.0)
        cnt = jnp.zeros_like(s1)
        for a in range(PEER_TOPK):
            cnt = jnp.where(s1 == v1[a:a + 1], cnt_sorted[a:a + 1], cnt)
        rank = jnp.full_like(s2, float(PEER_TOPK))
        for b in reversed(range(PEER_TOPK)):
            rank = jnp.where(s2 >= v2[b:b + 1], float(b), rank)
        cnt_ref[0, h] = cnt * RANK_SCALE
        a_ref[0, h] = jnp.exp(s1 - v1[0:1]) * (GELU_SCALE / zsum)
        rank_ref[0, h] = (rank * RANK_SCALE).astype(BF16)
        b_ref[0, h] = jnp.exp(s2 - v2[0:1]).astype(BF16)


def _route(qp, k1, k2):
    T = qp.shape[1]
    half = PEER_KEY_DIM // 2
    blk = pl.BlockSpec((1, PEER_HEADS, N_KEYS, LANES), lambda i: (i, 0, 0, 0))
    shape = lambda dt: jax.ShapeDtypeStruct((T // LANES, PEER_HEADS, N_KEYS, LANES), dt)
    keys = pl.BlockSpec((PEER_HEADS, N_KEYS, half), lambda i: (0, 0, 0))
    return pl.pallas_call(
        _route_kernel,
        grid=(T // ROUTE_TB,),
        in_specs=[pl.BlockSpec((QP_SLOTS, ROUTE_TB, half), lambda i: (0, i, 0)), keys, keys],
        out_specs=[blk, blk, blk, blk],
        out_shape=[shape(F32), shape(F32), shape(BF16), shape(BF16)],
        scratch_shapes=[pltpu.VMEM((N_CHAINS, N_KEYS, ROUTE_TB), F32),
                        pltpu.VMEM((N_CHAINS, PEER_TOPK, ROUTE_TB), F32),
                        pltpu.VMEM((PEER_HEADS, CAND_ROWS, ROUTE_TB), F32),
                        pltpu.VMEM((PEER_HEADS, PEER_TOPK, ROUTE_TB), F32)],
        compiler_params=_cparams(("parallel",)),
        name="route",
    )(qp, k1, k2)


PEER_TB = 512
PEER_TE = 2048
PEER_ROWS = PEER_TE // N_KEYS
PEER_NC = PEER_TB // LANES
INV_SQRT2 = 1.0 / math.sqrt(2.0)
GELU_SCALE = 0.5 * math.sqrt(2.0)
RANK_SCALE = 256.0


def _row_tile(ref, c, h, ii):
    return jnp.broadcast_to(ref[c, h, ii:ii + 1, :], (N_KEYS, LANES)).astype(BF16)


def _peer_kernel(hn_ref, u_ref, vt_ref, cnt_ref, a_ref, rank_ref, b_ref, h_ref, o_ref, y_ref, act_ref, wt_ref):
    e = pl.program_id(1)

    @pl.when(e == 0)
    def _():
        y_ref[...] = jnp.zeros_like(y_ref)

    act = _dot_nt(u_ref[...], hn_ref[...])
    for c in range(PEER_NC):
        act_ref[c] = act[:, c * LANES:(c + 1) * LANES].astype(BF16)

    def chunk(c, carry):
        for ii in range(PEER_ROWS):
            rs = slice(ii * N_KEYS, (ii + 1) * N_KEYS)
            w = jnp.zeros((N_KEYS, LANES), BF16)
            for h in range(PEER_HEADS):
                cnt = _row_tile(cnt_ref, c, h, ii)
                wa = _row_tile(a_ref, c, h, ii)
                sel = jnp.minimum(jnp.maximum(cnt - rank_ref[c, h], 0), b_ref[c, h])
                w = w + sel * wa
            x = act_ref[c, rs, :]
            wt_ref[c, rs, :] = x * (1 + lax.erf(x)) * w
        return carry
    lax.fori_loop(0, PEER_NC, chunk, 0)

    wt = jnp.concatenate([wt_ref[c] for c in range(PEER_NC)], axis=1)
    y_ref[...] += _dot(vt_ref[...], wt)

    @pl.when(e == pl.num_programs(1) - 1)
    def _():
        o_ref[...] = h_ref[...] + y_ref[...].T


def _peer(hn, u, vt, cnt, a, rank, b, h):
    T = hn.shape[0]
    rows = pl.BlockSpec((PEER_NC, PEER_HEADS, PEER_ROWS, LANES), lambda t, e: (t, 0, e, 0))
    keys = pl.BlockSpec((PEER_NC, PEER_HEADS, N_KEYS, LANES), lambda t, e: (t, 0, 0, 0))
    tok = pl.BlockSpec((PEER_TB, D_MODEL), lambda t, e: (t, 0))
    return pl.pallas_call(
        _peer_kernel,
        grid=(T // PEER_TB, N_EXPERTS // PEER_TE),
        in_specs=[tok,
                  pl.BlockSpec((PEER_TE, D_MODEL), lambda t, e: (e, 0)),
                  pl.BlockSpec((D_MODEL, PEER_TE), lambda t, e: (0, e)),
                  rows, rows, keys, keys, tok],
        out_specs=tok,
        out_shape=jax.ShapeDtypeStruct((T, D_MODEL), F32),
        scratch_shapes=[pltpu.VMEM((D_MODEL, PEER_TB), F32),
                        pltpu.VMEM((PEER_NC, PEER_TE, LANES), BF16),
                        pltpu.VMEM((PEER_NC, PEER_TE, LANES), BF16)],
        compiler_params=_cparams(("parallel", "arbitrary")),
        name="peer",
    )(hn, u, vt, cnt, a, rank, b, h)


def _regroup_w_in(w):
    o_kv = Q_LORA_RANK
    o_pe = o_kv + KV_LORA_RANK
    o_hy = o_pe + QK_ROPE_DIM
    o_gate = o_hy + 3 * HYENA_WIDTH
    out = jnp.zeros((D_MODEL, PROJ_COLS), BF16)
    out = out.at[:, COL_CQ:COL_CQ + Q_LORA_RANK].set(w[:, :o_kv].astype(BF16))
    out = out.at[:, COL_CKV:COL_CKV + KV_LORA_RANK].set(w[:, o_kv:o_pe].astype(BF16))
    out = out.at[:, COL_KPE + KPE_LANE:COL_KPE + KPE_LANE + QK_ROPE_DIM].set(w[:, o_pe:o_hy].astype(BF16))
    out = out.at[:, COL_HY:COL_HY + 3 * HYENA_WIDTH].set(w[:, o_hy:o_gate].astype(BF16))
    out = out.at[:, COL_GATE:].set(w[:, o_gate:].astype(BF16))
    return out


def _head_slots(w, width):
    k = w.shape[0]
    w3 = w.reshape(k, MLA_HEADS, width).astype(BF16)
    return jnp.zeros((k, MLA_HEADS, HEAD_SLOT), BF16).at[:, :, :width].set(w3).reshape(k, HW)


def kernel(x, attn_norm, w_in, b_gate, q_a_norm, w_uq, kv_a_norm, w_ukv, q_norm, k_norm, w_o_attn, hyena_conv_w, hyena_conv_b, filt_w1, filt_b1, filt_w2, filt_b2, filt_w3, filt_b3, filt_w4, filt_b4, filt_freq, hyena_bias, w_o_hyena, w_out, ffn_norm, peer_w_q, peer_keys1, peer_keys2, expert_u, expert_v):
    B = x.shape[0]
    T = B * SEQ
    x2 = x.reshape(T, D_MODEL)
    bf = lambda a: a.astype(BF16)

    proj = _inproj(x2, attn_norm, _regroup_w_in(w_in[0]))
    kv3 = w_ukv[0].reshape(KV_LORA_RANK, MLA_HEADS, QK_NOPE_DIM + V_HEAD_DIM)
    wuk_p = _head_slots(kv3[:, :, :QK_NOPE_DIM].reshape(KV_LORA_RANK, -1), QK_NOPE_DIM)
    wuv_p = _head_slots(kv3[:, :, QK_NOPE_DIM:].reshape(KV_LORA_RANK, -1), V_HEAD_DIM)
    vone = jnp.zeros((MLA_HEADS, HEAD_SLOT), BF16).at[:, V_HEAD_DIM].set(1.0).reshape(HW, 1)
    wuvt_p = jnp.concatenate([wuv_p.T, _pad2(vone, HW, KV_LORA_RANK)], axis=1)
    wuq_p = _head_slots(w_uq[0], QK_HEAD_DIM)
    wuqr_p = _rope_partner(wuq_p.reshape(Q_LORA_RANK, MLA_HEADS, HEAD_SLOT), axis=2).reshape(Q_LORA_RANK, HW)
    gq = _pad2(q_norm, 1, HEAD_SLOT)
    q, k, vt = _mlaprep(proj, q_a_norm, kv_a_norm, wuq_p, wuqr_p, wuk_p, wuvt_p,
                        gq, _rope_partner(gq, axis=1), _pad2(k_norm, 1, HEAD_SLOT))
    att = _attention(q, k, vt, B).reshape(T, MLA_HEADS * V_HEAD_DIM)

    wf, wft = (jnp.asarray(w).astype(BF16) for w in _dft_tables())
    hsd = _filter(filt_w1[0], filt_b1[0], filt_w2[0], filt_b2[0], filt_w3[0], filt_b3[0], filt_w4[0],
                  filt_b4[0], filt_freq[0])
    kp = _kspec(wf, hsd)
    ys, p1, p2 = _hyfwd(proj.reshape(B, SEQ, PROJ_COLS), hyena_conv_w[0], hyena_conv_b, hyena_bias, wf, kp)
    hy = _hyinv(wft, ys, p1, p2).reshape(T, HYENA_WIDTH)

    h, hn, qp = _merge(att, hy, proj, b_gate, x2, bf(w_o_attn[0]), bf(w_o_hyena[0]), bf(w_out[0]),
                       ffn_norm, bf(peer_w_q[0]))

    cnt, a, rank, b = _route(qp, bf(peer_keys1[0]), bf(peer_keys2[0]))
    out = _peer(hn, bf(expert_u[0]), bf(expert_v[0]).T, cnt, a, rank, b, h)
    return out.reshape(B, SEQ, D_MODEL)
```

```python
import functools
import math

import numpy as np
import jax
import jax.numpy as jnp
from jax import lax
from jax.experimental import pallas as pl
from jax.experimental.pallas import tpu as pltpu

F32 = jnp.float32
BF16 = jnp.bfloat16

D_MODEL = 1024
SEQ = 2048
MLA_HEADS = 8
QK_NOPE_DIM = 64
QK_ROPE_DIM = 32
QK_HEAD_DIM = QK_NOPE_DIM + QK_ROPE_DIM
V_HEAD_DIM = 64
Q_LORA_RANK = 256
KV_LORA_RANK = 128
ROPE_THETA = 10000.0
HYENA_WIDTH = 512
FILTER_EMB = 33
FILTER_ORDER = 64
FAST_DECAY_PCT = 0.3
SLOW_DECAY_PCT = 1.5
DECAY_TARGET = 1e-2
PEER_HEADS = 8
N_KEYS = 128
N_EXPERTS = N_KEYS * N_KEYS
PEER_KEY_DIM = 256
PEER_TOPK = 16
EPS = 1e-6

LANES = 128
HEAD_SLOT = LANES
FFT_N = 2 * SEQ
PROJ_COLS = 4096
COL_CQ, COL_CKV, COL_KPE, COL_HY, COL_GATE = 0, 256, 384, 512, 2048
KPE_LANE = QK_NOPE_DIM
VMEM_LIMIT = 56 * 1024 * 1024


def _cparams(sem):
    return pltpu.CompilerParams(dimension_semantics=sem, vmem_limit_bytes=VMEM_LIMIT)


def _dot(a, b):
    return jnp.dot(a, b, preferred_element_type=F32)


def _dot_nt(a, b):
    return lax.dot_general(a, b, (((1,), (1,)), ((), ())), preferred_element_type=F32)


def _rms(x, g):
    return x * lax.rsqrt(jnp.mean(x * x, axis=-1, keepdims=True) + EPS) * g


@functools.lru_cache(maxsize=None)
def _dft_tables():
    p = np.arange(FFT_N, dtype=np.int64)[:, None]
    s = np.arange(SEQ, dtype=np.int64)[None, :]
    f = np.where(p <= SEQ, p, p - SEQ)
    ang = (2.0 * np.pi / FFT_N) * ((f * s) % FFT_N).astype(np.float64)
    w = np.where(p <= SEQ, np.cos(ang), np.sin(ang)).astype(np.float32)
    return w, np.ascontiguousarray(w.T)


@functools.lru_cache(maxsize=None)
def _rotary_tables():
    half = QK_ROPE_DIM // 2
    inv_freq = ROPE_THETA ** (-np.arange(half, dtype=np.float64) / half)
    ang = np.arange(SEQ, dtype=np.float64)[:, None] * inv_freq[None, :]
    cos = np.ones((SEQ, HEAD_SLOT)); sa = np.zeros((SEQ, HEAD_SLOT)); sb = np.zeros((SEQ, HEAD_SLOT))
    lo, mid, hi = QK_NOPE_DIM, QK_NOPE_DIM + half, QK_HEAD_DIM
    cos[:, lo:mid] = np.cos(ang); cos[:, mid:hi] = np.cos(ang)
    sb[:, lo:mid] = -np.sin(ang)
    sa[:, mid:hi] = np.sin(ang)
    return tuple(t.astype(np.float32) for t in (cos, sa, sb))


@functools.lru_cache(maxsize=None)
def _filter_consts():
    L = SEQ
    t = np.linspace(0.0, 1.0, L)[:, None]
    bands = (FILTER_EMB - 1) // 2
    w = 2.0 * math.pi * np.arange(L)[:, None] / L
    f = np.linspace(1e-4, bands - 1, bands)[None, :]
    z = np.concatenate([t, np.cos(f * w), -np.sin(f * w)], axis=-1)
    zp = np.zeros((L, LANES)); zp[:, :FILTER_EMB] = z
    min_decay = math.log(DECAY_TARGET) / SLOW_DECAY_PCT
    max_decay = math.log(DECAY_TARGET) / FAST_DECAY_PCT
    deltas = np.abs(np.linspace(min_decay, max_decay, HYENA_WIDTH))[None, :]
    return zp.astype(np.float32), deltas.astype(np.float32)


INPROJ_TM = 1024
INPROJ_CW = 1024


def _inproj_kernel(x_ref, g_ref, w_ref, o_ref):
    xn = _rms(x_ref[...], g_ref[...]).astype(BF16)
    for c in range(PROJ_COLS // INPROJ_CW):
        sl = slice(c * INPROJ_CW, (c + 1) * INPROJ_CW)
        o_ref[:, sl] = _dot(xn, w_ref[:, sl]).astype(BF16)


def _inproj(x2, g, w_p):
    T = x2.shape[0]
    return pl.pallas_call(
        _inproj_kernel,
        grid=(T // INPROJ_TM,),
        in_specs=[pl.BlockSpec((INPROJ_TM, D_MODEL), lambda i: (i, 0)),
                  pl.BlockSpec((1, D_MODEL), lambda i: (0, 0)),
                  pl.BlockSpec((D_MODEL, PROJ_COLS), lambda i: (0, 0))],
        out_specs=pl.BlockSpec((INPROJ_TM, PROJ_COLS), lambda i: (i, 0)),
        out_shape=jax.ShapeDtypeStruct((T, PROJ_COLS), BF16),
        compiler_params=_cparams(("parallel",)),
        name="inproj",
    )(x2, g, w_p)


MLA_TM = 1024
HW = MLA_HEADS * HEAD_SLOT


def _rope_partner(a, axis):
    half = QK_ROPE_DIM // 2
    lo, mid, hi = QK_NOPE_DIM, QK_NOPE_DIM + half, QK_HEAD_DIM
    take = lambda s, e: lax.slice_in_dim(a, s, e, axis=axis)
    pad = [(0, 0)] * a.ndim
    pad[axis] = (lo, a.shape[axis] - hi)
    return jnp.pad(jnp.concatenate([take(mid, hi), take(lo, mid)], axis=axis), pad)


def _mlaprep_kernel(p_ref, qa_ref, kva_ref, wuq_ref, wuqr_ref, wuk_ref, wuvt_ref, gq_ref, gqr_ref, gk_ref,
                    ones_ref, cos_ref, sa_ref, sb_ref, q_ref, k_ref, vt_ref):
    p = p_ref[...].astype(F32)
    cqn = _rms(p[:, COL_CQ:COL_CQ + Q_LORA_RANK], qa_ref[...]).astype(BF16)
    ckvn = _rms(p[:, COL_CKV:COL_CKV + KV_LORA_RANK], kva_ref[...]).astype(BF16)
    kpe = p[:, COL_KPE:COL_KPE + HEAD_SLOT]
    q = _dot(cqn, wuq_ref[...])
    qr = _dot(cqn, wuqr_ref[...])
    kn = _dot(ckvn, wuk_ref[...])
    lane = lax.broadcasted_iota(jnp.int32, ckvn.shape, 1)
    ckv_one = jnp.concatenate([ckvn, jnp.where(lane == 0, 1.0, 0.0).astype(BF16)], axis=1)
    vt_ref[...] = _dot_nt(wuvt_ref[...], ckv_one).astype(BF16)
    ones = ones_ref[...]
    pair = 2 * HEAD_SLOT

    def head_sums(x):
        sq = (x * x).astype(BF16)
        return jnp.concatenate([_dot(sq[:, i:i + pair], ones) for i in range(0, HW, pair)], axis=1)
    ssq = head_sums(q)
    ssk = head_sums(kn)
    sspe = _dot((kpe * kpe).astype(BF16), ones_ref[:HEAD_SLOT, :HEAD_SLOT])
    cos, sa, sb = cos_ref[...], sa_ref[...], sb_ref[...]
    scale = QK_HEAD_DIM ** -0.5 * math.log2(math.e)
    inv_d = 1.0 / QK_HEAD_DIM
    tq = cos * (gq_ref[...] * scale)
    tqr = (sa + sb) * (gqr_ref[...] * scale)
    tk = cos * gk_ref[...]
    kg = kpe * gk_ref[...]
    half = QK_ROPE_DIM // 2
    rk = pltpu.roll(kg, half, 1) * sa + pltpu.roll(kg, HEAD_SLOT - half, 1) * sb
    for h in range(MLA_HEADS):
        sl = slice(h * HEAD_SLOT, (h + 1) * HEAD_SLOT)
        inv_q = lax.rsqrt(ssq[:, sl] * inv_d + EPS)
        q_ref[:, sl] = ((q[:, sl] * tq + qr[:, sl] * tqr) * inv_q).astype(BF16)
        inv_k = lax.rsqrt((ssk[:, sl] + sspe) * inv_d + EPS)
        k_ref[:, sl] = (((kn[:, sl] + kpe) * tk + rk) * inv_k).astype(BF16)


def _mlaprep(proj, qa, kva, wuq_p, wuqr_p, wuk_p, wuvt_p, gq, gqr, gk):
    T = proj.shape[0]
    cos, sa, sb = (jnp.asarray(t) for t in _rotary_tables())
    head_of = np.arange(2 * HEAD_SLOT) // HEAD_SLOT
    ones = jnp.asarray(head_of[:, None] == head_of[None, :], dtype=BF16)
    nseq = SEQ // MLA_TM
    full = lambda shape: pl.BlockSpec(shape, lambda i: (0, 0))
    pos = pl.BlockSpec((MLA_TM, HEAD_SLOT), lambda i: (i % nseq, 0))
    out = pl.BlockSpec((MLA_TM, HW), lambda i: (i, 0))
    return pl.pallas_call(
        _mlaprep_kernel,
        grid=(T // MLA_TM,),
        in_specs=[pl.BlockSpec((MLA_TM, 512), lambda i: (i, 0)),
                  full((1, Q_LORA_RANK)), full((1, KV_LORA_RANK)),
                  full((Q_LORA_RANK, HW)), full((Q_LORA_RANK, HW)), full((KV_LORA_RANK, HW)),
                  full((HW, 2 * KV_LORA_RANK)),
                  full((1, HEAD_SLOT)), full((1, HEAD_SLOT)), full((1, HEAD_SLOT)),
                  full((2 * HEAD_SLOT, 2 * HEAD_SLOT)), pos, pos, pos],
        out_specs=[out, out, pl.BlockSpec((HW, MLA_TM), lambda i: (0, i))],
        out_shape=[jax.ShapeDtypeStruct((T, HW), BF16)] * 2 + [jax.ShapeDtypeStruct((HW, T), BF16)],
        compiler_params=_cparams(("parallel",)),
        name="mlaprep",
    )(proj, qa, kva, wuq_p, wuqr_p, wuk_p, wuvt_p, gq, gqr, gk, ones, cos, sa, sb)


ATT_TQ = SEQ
ATT_HPS = 4


def _attn_kernel(q_ref, k_ref, vt_ref, o_ref):
    outs = []
    for hh in range(ATT_HPS):
        sl = slice(hh * HEAD_SLOT, (hh + 1) * HEAD_SLOT)
        st = _dot_nt(k_ref[0, :, sl], q_ref[0, :, sl])
        m = jnp.max(st, axis=0, keepdims=True)
        p = jnp.exp2(st - m).astype(BF16)
        ot = _dot(vt_ref[sl, :], p)
        on = ot[:V_HEAD_DIM, :] * (1.0 / ot[V_HEAD_DIM:V_HEAD_DIM + 1, :])
        outs.append(on.T)
    o_ref[0] = jnp.concatenate(outs, axis=-1).astype(BF16)


def _attention(q, k, vt, B):
    q3, k3 = (a.reshape(B, SEQ, HW) for a in (q, k))
    pair = ATT_HPS * HEAD_SLOT
    return pl.pallas_call(
        _attn_kernel,
        grid=(B, MLA_HEADS // ATT_HPS, SEQ // ATT_TQ),
        in_specs=[pl.BlockSpec((1, ATT_TQ, pair), lambda b, h, i: (b, i, h)),
                  pl.BlockSpec((1, SEQ, pair), lambda b, h, i: (b, 0, h)),
                  pl.BlockSpec((pair, SEQ), lambda b, h, i: (h, b))],
        out_specs=pl.BlockSpec((1, ATT_TQ, ATT_HPS * V_HEAD_DIM), lambda b, h, i: (b, i, h)),
        out_shape=jax.ShapeDtypeStruct((B, SEQ, MLA_HEADS * V_HEAD_DIM), BF16),
        compiler_params=_cparams(("parallel", "parallel", "parallel")),
        name="attn",
    )(q3, k3, vt)


FILT_TL = 512


def _filter_kernel(z_ref, w1_ref, b1_ref, w2_ref, b2_ref, w3_ref, b3_ref, w4_ref, b4_ref, fr_ref,
                   dl_ref, o_ref):
    hp = functools.partial(jnp.dot, precision=lax.Precision.HIGHEST, preferred_element_type=F32)
    z = z_ref[...]
    fr = fr_ref[...]
    h = jnp.sin(fr * (hp(z, w1_ref[...]) + b1_ref[...]))
    h = jnp.sin(fr * (hp(h, w2_ref[...]) + b2_ref[...]))
    h = jnp.sin(fr * (hp(h, w3_ref[...]) + b3_ref[...]))
    h4 = hp(h, w4_ref[...]) + b4_ref[...]
    decay = jnp.exp(-z[:, 0:1] * dl_ref[...])
    hf = h4[:, :HYENA_WIDTH] * decay
    hb = h4[:, HYENA_WIDTH:] * decay
    row = pl.program_id(0) * FILT_TL + lax.broadcasted_iota(jnp.int32, hb.shape, 0)
    hb = jnp.where(row == 0, 0.0, hb)
    o_ref[:, :HYENA_WIDTH] = (hf + hb).astype(BF16)
    o_ref[:, HYENA_WIDTH:] = (hf - hb).astype(BF16)


def _pad2(a, r, c):
    return jnp.zeros((r, c), a.dtype).at[:a.shape[0], :a.shape[1]].set(a)


def _filter(w1, b1, w2, b2, w3, b3, w4, b4, freq):
    z, deltas = (jnp.asarray(t) for t in _filter_consts())
    P = LANES
    args = (z, _pad2(w1, P, P), _pad2(b1[None], 1, P), _pad2(w2, P, P), _pad2(b2[None], 1, P),
            _pad2(w3, P, P), _pad2(b3[None], 1, P), _pad2(w4, P, 2 * HYENA_WIDTH), b4[None],
            _pad2(freq[None], 1, P), deltas)
    full = lambda a: pl.BlockSpec(a.shape, lambda i: (0, 0))
    return pl.pallas_call(
        _filter_kernel,
        grid=(SEQ // FILT_TL,),
        in_specs=[pl.BlockSpec((FILT_TL, P), lambda i: (i, 0))] + [full(a) for a in args[1:]],
        out_specs=pl.BlockSpec((FILT_TL, 2 * HYENA_WIDTH), lambda i: (i, 0)),
        out_shape=jax.ShapeDtypeStruct((SEQ, 2 * HYENA_WIDTH), BF16),
        compiler_params=_cparams(("parallel",)),
        name="filt",
    )(*args)


SPEC_TR = 512


def _kspec_kernel(w_ref, h_ref, o_ref):
    r = _dot(w_ref[...], h_ref[...])
    row = pl.program_id(0) * SPEC_TR + lax.broadcasted_iota(jnp.int32, (SPEC_TR, HYENA_WIDTH), 0)
    o_ref[...] = jnp.where(row <= SEQ, r[:, :HYENA_WIDTH], r[:, HYENA_WIDTH:])


def _kspec(wf, hsd):
    return pl.pallas_call(
        _kspec_kernel,
        grid=(FFT_N // SPEC_TR,),
        in_specs=[pl.BlockSpec((SPEC_TR, SEQ), lambda i: (i, 0)),
                  pl.BlockSpec((SEQ, 2 * HYENA_WIDTH), lambda i: (0, 0))],
        out_specs=pl.BlockSpec((SPEC_TR, HYENA_WIDTH), lambda i: (i, 0)),
        out_shape=jax.ShapeDtypeStruct((FFT_N, HYENA_WIDTH), F32),
        compiler_params=_cparams(("parallel",)),
        name="kspec",
    )(wf, hsd)


def _short_conv(u, w, b):
    row = lax.broadcasted_iota(jnp.int32, u.shape, 0)
    prev = jnp.where(row == 0, 0.0, pltpu.roll(u, 1, 0))
    nxt = jnp.where(row == SEQ - 1, 0.0, pltpu.roll(u, SEQ - 1, 0))
    return prev * w[0:1] + u * w[1:2] + nxt * w[2:3] + b


def _hyconv_kernel(x0_ref, x1_ref, v_ref, cw_ref, cb_ref, bias_ref, z_ref, p1_ref, p2_ref):
    C = HYENA_WIDTH
    cw = cw_ref[...]
    cb = cb_ref[...]
    x0 = _short_conv(x0_ref[0].astype(F32), cw[:, 0:C], cb[:, 0:C])
    x1 = _short_conv(x1_ref[0].astype(F32), cw[:, C:2 * C], cb[:, C:2 * C])
    v = _short_conv(v_ref[0].astype(F32), cw[:, 2 * C:], cb[:, 2 * C:])
    z = v * x1
    z_ref[0] = z.astype(BF16)
    p1_ref[0] = x0.astype(BF16)
    p2_ref[0] = (x0 * z * bias_ref[...]).astype(BF16)


def _hyconv(proj3, cw, cb, bias):
    B = proj3.shape[0]
    cblk = lambda c: pl.BlockSpec((1, SEQ, HYENA_WIDTH), lambda b: (b, 0, c))
    full = lambda a: pl.BlockSpec(a.shape, lambda b: (0, 0))
    out = pl.BlockSpec((1, SEQ, HYENA_WIDTH), lambda b: (b, 0, 0))
    return pl.pallas_call(
        _hyconv_kernel,
        grid=(B,),
        in_specs=[cblk(COL_HY // HYENA_WIDTH), cblk(COL_HY // HYENA_WIDTH + 1), cblk(COL_HY // HYENA_WIDTH + 2),
                  full(cw), full(cb), full(bias)],
        out_specs=[out, out, out],
        out_shape=[jax.ShapeDtypeStruct((B, SEQ, HYENA_WIDTH), BF16)] * 3,
        compiler_params=_cparams(("parallel",)),
        name="hyconv",
    )(proj3, proj3, proj3, cw, cb, bias)


def _hyfwd_kernel(z_ref, wre_ref, wim_ref, kre_ref, kim_ref, y_ref):
    r = pl.program_id(0)
    z = z_ref[0]
    a = _dot(wre_ref[...], z)
    q = _dot(wim_ref[...], z)
    ka = kre_ref[...]
    kq = kim_ref[...]
    first = jnp.logical_and(r == 0, lax.broadcasted_iota(jnp.int32, a.shape, 0) == 0)
    yr = jnp.where(first, a * ka, a * ka - q * kq)
    yq = jnp.where(first, q * kq, a * kq + q * ka)
    sc = jnp.where(first, 1.0 / FFT_N, 2.0 / FFT_N)
    y_ref[0, 0] = (yr * sc).astype(BF16)
    y_ref[0, 1] = (yq * sc).astype(BF16)


HY_TR = 1024


def _hyfwd(z, wf, kp):
    B = z.shape[0]
    nr = SEQ // HY_TR
    y = pl.pallas_call(
        _hyfwd_kernel,
        grid=(nr, B),
        in_specs=[pl.BlockSpec((1, SEQ, HYENA_WIDTH), lambda r, b: (b, 0, 0)),
                  pl.BlockSpec((HY_TR, SEQ), lambda r, b: (r, 0)),
                  pl.BlockSpec((HY_TR, SEQ), lambda r, b: (r + nr, 0)),
                  pl.BlockSpec((HY_TR, HYENA_WIDTH), lambda r, b: (r, 0)),
                  pl.BlockSpec((HY_TR, HYENA_WIDTH), lambda r, b: (r + nr, 0))],
        out_specs=pl.BlockSpec((1, 2, HY_TR, HYENA_WIDTH), lambda r, b: (b, 0, r, 0)),
        out_shape=jax.ShapeDtypeStruct((B, 2, SEQ, HYENA_WIDTH), BF16),
        compiler_params=_cparams(("parallel", "parallel")),
        name="hyfwd",
    )(z, wf, wf, kp, kp)
    return y.reshape(B, FFT_N, HYENA_WIDTH)


HYI_TT = 1024


def _hyinv_kernel(wt_ref, y_ref, p1_ref, p2_ref, o_ref):
    conv = _dot(wt_ref[...], y_ref[0])
    o_ref[0] = (p1_ref[0].astype(F32) * conv + p2_ref[0].astype(F32)).astype(BF16)


def _hyinv(wft, ys, p1, p2):
    B = ys.shape[0]
    blk = pl.BlockSpec((1, HYI_TT, HYENA_WIDTH), lambda t, b: (b, t, 0))
    return pl.pallas_call(
        _hyinv_kernel,
        grid=(SEQ // HYI_TT, B),
        in_specs=[pl.BlockSpec((HYI_TT, FFT_N), lambda t, b: (t, 0)),
                  pl.BlockSpec((1, FFT_N, HYENA_WIDTH), lambda t, b: (b, 0, 0)),
                  blk, blk],
        out_specs=blk,
        out_shape=jax.ShapeDtypeStruct((B, SEQ, HYENA_WIDTH), BF16),
        compiler_params=_cparams(("parallel", "parallel")),
        name="hyinv",
    )(wft, ys, p1, p2)


MERGE_TM = 1024
QP_SLOTS = PEER_HEADS * 2


def _merge_kernel(att_ref, hy_ref, gl_ref, bg_ref, x_ref, woa_ref, woh_ref, wout_ref, fg_ref, wq_ref,
                  h_ref, hn_ref, qp_ref):
    a = _dot(att_ref[...], woa_ref[...])
    yh = _dot(hy_ref[...], woh_ref[...])
    g = 1.0 / (1.0 + jnp.exp(-(gl_ref[...].astype(F32) + bg_ref[...])))
    merged = (g[:, :D_MODEL] * a + g[:, D_MODEL:] * yh).astype(BF16)
    h = x_ref[...] + _dot(merged, wout_ref[...])
    h_ref[...] = h
    hn = _rms(h, fg_ref[...])
    hn_ref[...] = (hn * INV_SQRT2).astype(BF16)
    qp = _dot(hn.astype(BF16), wq_ref[...])
    for c in range(QP_SLOTS):
        qp_ref[c] = qp[:, c * LANES:(c + 1) * LANES].astype(BF16)


def _merge(att, hy, proj, bg, x2, woa, woh, wout, fg, wq):
    T = x2.shape[0]
    half = PEER_KEY_DIM // 2
    row = lambda w: pl.BlockSpec((MERGE_TM, w), lambda i: (i, 0))
    full = lambda a: pl.BlockSpec(a.shape, lambda i: (0, 0))
    return pl.pallas_call(
        _merge_kernel,
        grid=(T // MERGE_TM,),
        in_specs=[row(att.shape[1]), row(hy.shape[1]),
                  pl.BlockSpec((MERGE_TM, 2 * D_MODEL), lambda i: (i, COL_GATE // (2 * D_MODEL))),
                  full(bg), row(D_MODEL), full(woa), full(woh), full(wout), full(fg), full(wq)],
        out_specs=[row(D_MODEL), row(D_MODEL),
                   pl.BlockSpec((QP_SLOTS, MERGE_TM, half), lambda i: (0, i, 0))],
        out_shape=[jax.ShapeDtypeStruct((T, D_MODEL), F32),
                   jax.ShapeDtypeStruct((T, D_MODEL), BF16),
                   jax.ShapeDtypeStruct((QP_SLOTS, T, half), BF16)],
        compiler_params=_cparams(("parallel",)),
        name="merge",
    )(att, hy, proj, bg, x2, woa, woh, wout, fg, wq)


ROUTE_TB = LANES
NEG_INF = float("-inf")
N_CHAINS = 2 * PEER_HEADS


def _sort_pairs(n):
    pairs = []

    def merge(lo, hi, r):
        step = 2 * r
        if step < hi - lo:
            merge(lo, hi, step)
            merge(lo + r, hi, step)
            pairs.extend((i, i + r) for i in range(lo + r, hi - r, step))
        else:
            pairs.append((lo, lo + r))

    def sort(lo, hi):
        if hi > lo:
            mid = lo + (hi - lo) // 2
            sort(lo, mid)
            sort(mid + 1, hi)
            merge(lo, hi, 1)
    sort(0, n - 1)
    return pairs


SUBLANES = 8


def _top_sorted(s):
    n = s.shape[0] // SUBLANES
    v = [s[SUBLANES * i:SUBLANES * (i + 1), :] for i in range(n)]
    for i, j in _sort_pairs(n):
        v[i], v[j] = jnp.maximum(v[i], v[j]), jnp.minimum(v[i], v[j])
    out = []
    for k in range(PEER_TOPK):
        m = jnp.max(v[0], axis=0, keepdims=True)
        out.append(m)
        if k + 1 < PEER_TOPK:
            popped = v[0] == m
            for d in range(PEER_TOPK - 1 - k):
                v[d] = jnp.where(popped, v[d + 1], v[d])
    return jnp.concatenate(out, axis=0)


CAND_ROWS = 72


def _extract_top(work_ref, vals_ref, n_chains):
    slot = lax.broadcasted_iota(jnp.int32, vals_ref.shape[1:], 0)

    def body(k, carry):
        for c in range(n_chains):
            s = work_ref[c]
            m = jnp.max(s, axis=0, keepdims=True)
            work_ref[c] = jnp.where(s == m, NEG_INF, s)
            vals_ref[c] = jnp.where(slot == k, m, vals_ref[c])
        return carry
    lax.fori_loop(0, PEER_TOPK, body, 0)


def _candidates(v1, v2):
    a16 = lax.broadcasted_iota(jnp.int32, v1.shape, 0)
    a8 = a16[0:8]
    return jnp.concatenate([
        v1[0:1] + v2,
        v1[1:2] + v2[0:8], v1[2:3] + v2[0:8], v1[3:4] + v2[0:8],
        jnp.where(a16 >= 4, v1 + v2[0:1], NEG_INF),
        jnp.where(a8 >= 4, v1[0:8] + v2[1:2], NEG_INF),
        jnp.where(a8 == 4, v1[0:8] + v2[2:3], NEG_INF),
    ], axis=0)


def _route_kernel(q_ref, k1_ref, k2_ref, cnt_ref, a_ref, rank_ref, b_ref,
                  s_ref, vals_ref, cand_ref, cvals_ref):
    for h in range(PEER_HEADS):
        for side, k_ref in enumerate((k1_ref, k2_ref)):
            s = _dot_nt(k_ref[h], q_ref[2 * h + side])
            s_ref[2 * h + side] = s
            vals_ref[2 * h + side] = _top_sorted(s)
    for h in range(PEER_HEADS):
        cand_ref[h] = _candidates(vals_ref[2 * h], vals_ref[2 * h + 1])
    cvals_ref[...] = jnp.zeros_like(cvals_ref)
    _extract_top(cand_ref, cvals_ref, PEER_HEADS)

    for h in range(PEER_HEADS):
        v1, v2, cv = vals_ref[2 * h], vals_ref[2 * h + 1], cvals_ref[h]
        s1, s2 = s_ref[2 * h], s_ref[2 * h + 1]
        tau = cv[PEER_TOPK - 1:PEER_TOPK]
        zsum = jnp.sum(jnp.exp(cv - cv[0:1]), axis=0, keepdims=True)
        cnt_sorted = jnp.zeros_like(v1)
        for b in range(PEER_TOPK):
            cnt_sorted = cnt_sorted + jnp.where(v1 + v2[b:b + 1] >= tau, 1.0, 0.0)
        cnt = jnp.zeros_like(s1)
        for a in range(PEER_TOPK):
            cnt = jnp.where(s1 == v1[a:a + 1], cnt_sorted[a:a + 1], cnt)
        rank = jnp.full_like(s2, float(PEER_TOPK))
        for b in reversed(range(PEER_TOPK)):
            rank = jnp.where(s2 >= v2[b:b + 1], float(b), rank)
        cnt_ref[0, h] = cnt * RANK_SCALE
        a_ref[0, h] = jnp.exp(s1 - v1[0:1]) * (GELU_SCALE / zsum)
        rank_ref[0, h] = (rank * RANK_SCALE).astype(BF16)
        b_ref[0, h] = jnp.exp(s2 - v2[0:1]).astype(BF16)


def _route(qp, k1, k2):
    T = qp.shape[1]
    half = PEER_KEY_DIM // 2
    blk = pl.BlockSpec((1, PEER_HEADS, N_KEYS, LANES), lambda i: (i, 0, 0, 0))
    shape = lambda dt: jax.ShapeDtypeStruct((T // LANES, PEER_HEADS, N_KEYS, LANES), dt)
    keys = pl.BlockSpec((PEER_HEADS, N_KEYS, half), lambda i: (0, 0, 0))
    return pl.pallas_call(
        _route_kernel,
        grid=(T // ROUTE_TB,),
        in_specs=[pl.BlockSpec((QP_SLOTS, ROUTE_TB, half), lambda i: (0, i, 0)), keys, keys],
        out_specs=[blk, blk, blk, blk],
        out_shape=[shape(F32), shape(F32), shape(BF16), shape(BF16)],
        scratch_shapes=[pltpu.VMEM((N_CHAINS, N_KEYS, ROUTE_TB), F32),
                        pltpu.VMEM((N_CHAINS, PEER_TOPK, ROUTE_TB), F32),
                        pltpu.VMEM((PEER_HEADS, CAND_ROWS, ROUTE_TB), F32),
                        pltpu.VMEM((PEER_HEADS, PEER_TOPK, ROUTE_TB), F32)],
        compiler_params=_cparams(("parallel",)),
        name="route",
    )(qp, k1, k2)


PEER_TB = 512
PEER_TE = 2048
PEER_ROWS = PEER_TE // N_KEYS
PEER_NC = PEER_TB // LANES
INV_SQRT2 = 1.0 / math.sqrt(2.0)
GELU_SCALE = 0.5 * math.sqrt(2.0)
RANK_SCALE = 256.0


def _row_tile(ref, c, h, ii):
    return jnp.broadcast_to(ref[c, h, ii:ii + 1, :], (N_KEYS, LANES)).astype(BF16)


def _peer_kernel(hn_ref, u_ref, vt_ref, cnt_ref, a_ref, rank_ref, b_ref, h_ref, o_ref, y_ref, act_ref, wt_ref):
    e = pl.program_id(1)

    @pl.when(e == 0)
    def _():
        y_ref[...] = jnp.zeros_like(y_ref)

    act = _dot_nt(u_ref[...], hn_ref[...])
    for c in range(PEER_NC):
        act_ref[c] = act[:, c * LANES:(c + 1) * LANES].astype(BF16)

    def chunk(c, carry):
        for ii in range(PEER_ROWS):
            rs = slice(ii * N_KEYS, (ii + 1) * N_KEYS)
            w = jnp.zeros((N_KEYS, LANES), BF16)
            for h in range(PEER_HEADS):
                cnt = _row_tile(cnt_ref, c, h, ii)
                wa = _row_tile(a_ref, c, h, ii)
                sel = jnp.minimum(jnp.maximum(cnt - rank_ref[c, h], 0), b_ref[c, h])
                w = w + sel * wa
            x = act_ref[c, rs, :]
            wt_ref[c, rs, :] = x * (1 + lax.erf(x)) * w
        return carry
    lax.fori_loop(0, PEER_NC, chunk, 0)

    wt = jnp.concatenate([wt_ref[c] for c in range(PEER_NC)], axis=1)
    y_ref[...] += _dot(vt_ref[...], wt)

    @pl.when(e == pl.num_programs(1) - 1)
    def _():
        o_ref[...] = h_ref[...] + y_ref[...].T


def _peer(hn, u, vt, cnt, a, rank, b, h):
    T = hn.shape[0]
    rows = pl.BlockSpec((PEER_NC, PEER_HEADS, PEER_ROWS, LANES), lambda t, e: (t, 0, e, 0))
    keys = pl.BlockSpec((PEER_NC, PEER_HEADS, N_KEYS, LANES), lambda t, e: (t, 0, 0, 0))
    tok = pl.BlockSpec((PEER_TB, D_MODEL), lambda t, e: (t, 0))
    return pl.pallas_call(
        _peer_kernel,
        grid=(T // PEER_TB, N_EXPERTS // PEER_TE),
        in_specs=[tok,
                  pl.BlockSpec((PEER_TE, D_MODEL), lambda t, e: (e, 0)),
                  pl.BlockSpec((D_MODEL, PEER_TE), lambda t, e: (0, e)),
                  rows, rows, keys, keys, tok],
        out_specs=tok,
        out_shape=jax.ShapeDtypeStruct((T, D_MODEL), F32),
        scratch_shapes=[pltpu.VMEM((D_MODEL, PEER_TB), F32),
                        pltpu.VMEM((PEER_NC, PEER_TE, LANES), BF16),
                        pltpu.VMEM((PEER_NC, PEER_TE, LANES), BF16)],
        compiler_params=_cparams(("parallel", "arbitrary")),
        name="peer",
    )(hn, u, vt, cnt, a, rank, b, h)


def _regroup_w_in(w):
    o_kv = Q_LORA_RANK
    o_pe = o_kv + KV_LORA_RANK
    o_hy = o_pe + QK_ROPE_DIM
    o_gate = o_hy + 3 * HYENA_WIDTH
    out = jnp.zeros((D_MODEL, PROJ_COLS), BF16)
    out = out.at[:, COL_CQ:COL_CQ + Q_LORA_RANK].set(w[:, :o_kv].astype(BF16))
    out = out.at[:, COL_CKV:COL_CKV + KV_LORA_RANK].set(w[:, o_kv:o_pe].astype(BF16))
    out = out.at[:, COL_KPE + KPE_LANE:COL_KPE + KPE_LANE + QK_ROPE_DIM].set(w[:, o_pe:o_hy].astype(BF16))
    out = out.at[:, COL_HY:COL_HY + 3 * HYENA_WIDTH].set(w[:, o_hy:o_gate].astype(BF16))
    out = out.at[:, COL_GATE:].set(w[:, o_gate:].astype(BF16))
    return out


def _head_slots(w, width):
    k = w.shape[0]
    w3 = w.reshape(k, MLA_HEADS, width).astype(BF16)
    return jnp.zeros((k, MLA_HEADS, HEAD_SLOT), BF16).at[:, :, :width].set(w3).reshape(k, HW)


def kernel(x, attn_norm, w_in, b_gate, q_a_norm, w_uq, kv_a_norm, w_ukv, q_norm, k_norm, w_o_attn, hyena_conv_w, hyena_conv_b, filt_w1, filt_b1, filt_w2, filt_b2, filt_w3, filt_b3, filt_w4, filt_b4, filt_freq, hyena_bias, w_o_hyena, w_out, ffn_norm, peer_w_q, peer_keys1, peer_keys2, expert_u, expert_v):
    B = x.shape[0]
    T = B * SEQ
    x2 = x.reshape(T, D_MODEL)
    bf = lambda a: a.astype(BF16)

    proj = _inproj(x2, attn_norm, _regroup_w_in(w_in[0]))
    kv3 = w_ukv[0].reshape(KV_LORA_RANK, MLA_HEADS, QK_NOPE_DIM + V_HEAD_DIM)
    wuk_p = _head_slots(kv3[:, :, :QK_NOPE_DIM].reshape(KV_LORA_RANK, -1), QK_NOPE_DIM)
    wuv_p = _head_slots(kv3[:, :, QK_NOPE_DIM:].reshape(KV_LORA_RANK, -1), V_HEAD_DIM)
    vone = jnp.zeros((MLA_HEADS, HEAD_SLOT), BF16).at[:, V_HEAD_DIM].set(1.0).reshape(HW, 1)
    wuvt_p = jnp.concatenate([wuv_p.T, _pad2(vone, HW, KV_LORA_RANK)], axis=1)
    wuq_p = _head_slots(w_uq[0], QK_HEAD_DIM)
    wuqr_p = _rope_partner(wuq_p.reshape(Q_LORA_RANK, MLA_HEADS, HEAD_SLOT), axis=2).reshape(Q_LORA_RANK, HW)
    gq = _pad2(q_norm, 1, HEAD_SLOT)
    q, k, vt = _mlaprep(proj, q_a_norm, kv_a_norm, wuq_p, wuqr_p, wuk_p, wuvt_p,
                        gq, _rope_partner(gq, axis=1), _pad2(k_norm, 1, HEAD_SLOT))
    att = _attention(q, k, vt, B).reshape(T, MLA_HEADS * V_HEAD_DIM)

    wf, wft = (jnp.asarray(w).astype(BF16) for w in _dft_tables())
    hsd = _filter(filt_w1[0], filt_b1[0], filt_w2[0], filt_b2[0], filt_w3[0], filt_b3[0], filt_w4[0],
                  filt_b4[0], filt_freq[0])
    kp = _kspec(wf, hsd)
    z, p1, p2 = _hyconv(proj.reshape(B, SEQ, PROJ_COLS), hyena_conv_w[0], hyena_conv_b, hyena_bias)
    ys = _hyfwd(z, wf, kp)
    hy = _hyinv(wft, ys, p1, p2).reshape(T, HYENA_WIDTH)

    h, hn, qp = _merge(att, hy, proj, b_gate, x2, bf(w_o_attn[0]), bf(w_o_hyena[0]), bf(w_out[0]),
                       ffn_norm, bf(peer_w_q[0]))

    cnt, a, rank, b = _route(qp, bf(peer_keys1[0]), bf(peer_keys2[0]))
    out = _peer(hn, bf(expert_u[0]), bf(expert_v[0]).T, cnt, a, rank, b, h)
    return out.reshape(B, SEQ, D_MODEL)
```

```python
import functools
import math

import numpy as np
import jax
import jax.numpy as jnp
from jax import lax
from jax.experimental import pallas as pl
from jax.experimental.pallas import tpu as pltpu

F32 = jnp.float32
BF16 = jnp.bfloat16

D_MODEL = 1024
SEQ = 2048
MLA_HEADS = 8
QK_NOPE_DIM = 64
QK_ROPE_DIM = 32
QK_HEAD_DIM = QK_NOPE_DIM + QK_ROPE_DIM
V_HEAD_DIM = 64
Q_LORA_RANK = 256
KV_LORA_RANK = 128
ROPE_THETA = 10000.0
HYENA_WIDTH = 512
FILTER_EMB = 33
FILTER_ORDER = 64
FAST_DECAY_PCT = 0.3
SLOW_DECAY_PCT = 1.5
DECAY_TARGET = 1e-2
PEER_HEADS = 8
N_KEYS = 128
N_EXPERTS = N_KEYS * N_KEYS
PEER_KEY_DIM = 256
PEER_TOPK = 16
EPS = 1e-6

LANES = 128
HEAD_SLOT = LANES
FFT_N = 2 * SEQ
PROJ_COLS = 4096
COL_CQ, COL_CKV, COL_KPE, COL_HY, COL_GATE = 0, 256, 384, 512, 2048
KPE_LANE = QK_NOPE_DIM
VMEM_LIMIT = 56 * 1024 * 1024


def _cparams(sem):
    return pltpu.CompilerParams(dimension_semantics=sem, vmem_limit_bytes=VMEM_LIMIT)


def _dot(a, b):
    return jnp.dot(a, b, preferred_element_type=F32)


def _dot_nt(a, b):
    return lax.dot_general(a, b, (((1,), (1,)), ((), ())), preferred_element_type=F32)


def _rms(x, g):
    return x * lax.rsqrt(jnp.mean(x * x, axis=-1, keepdims=True) + EPS) * g


@functools.lru_cache(maxsize=None)
def _dft_tables():
    p = np.arange(FFT_N, dtype=np.int64)[:, None]
    s = np.arange(SEQ, dtype=np.int64)[None, :]
    f = np.where(p <= SEQ, p, p - SEQ)
    ang = (2.0 * np.pi / FFT_N) * ((f * s) % FFT_N).astype(np.float64)
    w = np.where(p <= SEQ, np.cos(ang), np.sin(ang)).astype(np.float32)
    return w, np.ascontiguousarray(w.T)


@functools.lru_cache(maxsize=None)
def _rotary_tables():
    half = QK_ROPE_DIM // 2
    inv_freq = ROPE_THETA ** (-np.arange(half, dtype=np.float64) / half)
    ang = np.arange(SEQ, dtype=np.float64)[:, None] * inv_freq[None, :]
    cos = np.ones((SEQ, HEAD_SLOT)); sa = np.zeros((SEQ, HEAD_SLOT)); sb = np.zeros((SEQ, HEAD_SLOT))
    lo, mid, hi = QK_NOPE_DIM, QK_NOPE_DIM + half, QK_HEAD_DIM
    cos[:, lo:mid] = np.cos(ang); cos[:, mid:hi] = np.cos(ang)
    sb[:, lo:mid] = -np.sin(ang)
    sa[:, mid:hi] = np.sin(ang)
    return tuple(t.astype(np.float32) for t in (cos, sa, sb))


@functools.lru_cache(maxsize=None)
def _filter_consts():
    L = SEQ
    t = np.linspace(0.0, 1.0, L)[:, None]
    bands = (FILTER_EMB - 1) // 2
    w = 2.0 * math.pi * np.arange(L)[:, None] / L
    f = np.linspace(1e-4, bands - 1, bands)[None, :]
    z = np.concatenate([t, np.cos(f * w), -np.sin(f * w)], axis=-1)
    zp = np.zeros((L, LANES)); zp[:, :FILTER_EMB] = z
    min_decay = math.log(DECAY_TARGET) / SLOW_DECAY_PCT
    max_decay = math.log(DECAY_TARGET) / FAST_DECAY_PCT
    deltas = np.abs(np.linspace(min_decay, max_decay, HYENA_WIDTH))[None, :]
    return zp.astype(np.float32), deltas.astype(np.float32)


INPROJ_TM = 1024
INPROJ_CW = 1024


def _inproj_kernel(x_ref, g_ref, w_ref, o_ref):
    xn = _rms(x_ref[...], g_ref[...]).astype(BF16)
    for c in range(PROJ_COLS // INPROJ_CW):
        sl = slice(c * INPROJ_CW, (c + 1) * INPROJ_CW)
        o_ref[:, sl] = _dot(xn, w_ref[:, sl]).astype(BF16)


def _inproj(x2, g, w_p):
    T = x2.shape[0]
    return pl.pallas_call(
        _inproj_kernel,
        grid=(T // INPROJ_TM,),
        in_specs=[pl.BlockSpec((INPROJ_TM, D_MODEL), lambda i: (i, 0)),
                  pl.BlockSpec((1, D_MODEL), lambda i: (0, 0)),
                  pl.BlockSpec((D_MODEL, PROJ_COLS), lambda i: (0, 0))],
        out_specs=pl.BlockSpec((INPROJ_TM, PROJ_COLS), lambda i: (i, 0)),
        out_shape=jax.ShapeDtypeStruct((T, PROJ_COLS), BF16),
        compiler_params=_cparams(("parallel",)),
        name="inproj",
    )(x2, g, w_p)


MLA_TM = 1024
HW = MLA_HEADS * HEAD_SLOT


def _rope_partner(a, axis):
    half = QK_ROPE_DIM // 2
    lo, mid, hi = QK_NOPE_DIM, QK_NOPE_DIM + half, QK_HEAD_DIM
    take = lambda s, e: lax.slice_in_dim(a, s, e, axis=axis)
    pad = [(0, 0)] * a.ndim
    pad[axis] = (lo, a.shape[axis] - hi)
    return jnp.pad(jnp.concatenate([take(mid, hi), take(lo, mid)], axis=axis), pad)


def _mlaprep_kernel(p_ref, qa_ref, kva_ref, wuq_ref, wuqr_ref, wuk_ref, wuvt_ref, gq_ref, gqr_ref, gk_ref,
                    ones_ref, cos_ref, sa_ref, sb_ref, q_ref, k_ref, vt_ref):
    p = p_ref[...].astype(F32)
    cqn = _rms(p[:, COL_CQ:COL_CQ + Q_LORA_RANK], qa_ref[...]).astype(BF16)
    ckvn = _rms(p[:, COL_CKV:COL_CKV + KV_LORA_RANK], kva_ref[...]).astype(BF16)
    kpe = p[:, COL_KPE:COL_KPE + HEAD_SLOT]
    q = _dot(cqn, wuq_ref[...])
    qr = _dot(cqn, wuqr_ref[...])
    kn = _dot(ckvn, wuk_ref[...])
    lane = lax.broadcasted_iota(jnp.int32, ckvn.shape, 1)
    ckv_one = jnp.concatenate([ckvn, jnp.where(lane == 0, 1.0, 0.0).astype(BF16)], axis=1)
    vt_ref[...] = _dot_nt(wuvt_ref[...], ckv_one).astype(BF16)
    ones = ones_ref[...]
    pair = 2 * HEAD_SLOT

    def head_sums(x):
        sq = (x * x).astype(BF16)
        return jnp.concatenate([_dot(sq[:, i:i + pair], ones) for i in range(0, HW, pair)], axis=1)
    ssq = head_sums(q)
    ssk = head_sums(kn)
    sspe = _dot((kpe * kpe).astype(BF16), ones_ref[:HEAD_SLOT, :HEAD_SLOT])
    cos, sa, sb = cos_ref[...], sa_ref[...], sb_ref[...]
    scale = QK_HEAD_DIM ** -0.5 * math.log2(math.e)
    inv_d = 1.0 / QK_HEAD_DIM
    tq = cos * (gq_ref[...] * scale)
    tqr = (sa + sb) * (gqr_ref[...] * scale)
    tk = cos * gk_ref[...]
    kg = kpe * gk_ref[...]
    half = QK_ROPE_DIM // 2
    rk = pltpu.roll(kg, half, 1) * sa + pltpu.roll(kg, HEAD_SLOT - half, 1) * sb
    for h in range(MLA_HEADS):
        sl = slice(h * HEAD_SLOT, (h + 1) * HEAD_SLOT)
        inv_q = lax.rsqrt(ssq[:, sl] * inv_d + EPS)
        q_ref[:, sl] = ((q[:, sl] * tq + qr[:, sl] * tqr) * inv_q).astype(BF16)
        inv_k = lax.rsqrt((ssk[:, sl] + sspe) * inv_d + EPS)
        k_ref[:, sl] = (((kn[:, sl] + kpe) * tk + rk) * inv_k).astype(BF16)


def _mlaprep(proj, qa, kva, wuq_p, wuqr_p, wuk_p, wuvt_p, gq, gqr, gk):
    T = proj.shape[0]
    cos, sa, sb = (jnp.asarray(t) for t in _rotary_tables())
    head_of = np.arange(2 * HEAD_SLOT) // HEAD_SLOT
    ones = jnp.asarray(head_of[:, None] == head_of[None, :], dtype=BF16)
    nseq = SEQ // MLA_TM
    full = lambda shape: pl.BlockSpec(shape, lambda i: (0, 0))
    pos = pl.BlockSpec((MLA_TM, HEAD_SLOT), lambda i: (i % nseq, 0))
    out = pl.BlockSpec((MLA_TM, HW), lambda i: (i, 0))
    return pl.pallas_call(
        _mlaprep_kernel,
        grid=(T // MLA_TM,),
        in_specs=[pl.BlockSpec((MLA_TM, 512), lambda i: (i, 0)),
                  full((1, Q_LORA_RANK)), full((1, KV_LORA_RANK)),
                  full((Q_LORA_RANK, HW)), full((Q_LORA_RANK, HW)), full((KV_LORA_RANK, HW)),
                  full((HW, 2 * KV_LORA_RANK)),
                  full((1, HEAD_SLOT)), full((1, HEAD_SLOT)), full((1, HEAD_SLOT)),
                  full((2 * HEAD_SLOT, 2 * HEAD_SLOT)), pos, pos, pos],
        out_specs=[out, out, pl.BlockSpec((HW, MLA_TM), lambda i: (0, i))],
        out_shape=[jax.ShapeDtypeStruct((T, HW), BF16)] * 2 + [jax.ShapeDtypeStruct((HW, T), BF16)],
        compiler_params=_cparams(("parallel",)),
        name="mlaprep",
    )(proj, qa, kva, wuq_p, wuqr_p, wuk_p, wuvt_p, gq, gqr, gk, ones, cos, sa, sb)


ATT_TQ = SEQ
ATT_HPS = 4


def _attn_kernel(q_ref, k_ref, vt_ref, o_ref):
    outs = []
    for hh in range(ATT_HPS):
        sl = slice(hh * HEAD_SLOT, (hh + 1) * HEAD_SLOT)
        st = _dot_nt(k_ref[0, :, sl], q_ref[0, :, sl])
        sb = st.astype(BF16)
        p = jnp.exp2(sb - jnp.max(sb, axis=0, keepdims=True))
        ot = _dot(vt_ref[sl, :], p)
        on = ot[:V_HEAD_DIM, :] * (1.0 / ot[V_HEAD_DIM:V_HEAD_DIM + 1, :])
        outs.append(on.T)
    o_ref[0] = jnp.concatenate(outs, axis=-1).astype(BF16)


def _attention(q, k, vt, B):
    q3, k3 = (a.reshape(B, SEQ, HW) for a in (q, k))
    pair = ATT_HPS * HEAD_SLOT
    return pl.pallas_call(
        _attn_kernel,
        grid=(B, MLA_HEADS // ATT_HPS, SEQ // ATT_TQ),
        in_specs=[pl.BlockSpec((1, ATT_TQ, pair), lambda b, h, i: (b, i, h)),
                  pl.BlockSpec((1, SEQ, pair), lambda b, h, i: (b, 0, h)),
                  pl.BlockSpec((pair, SEQ), lambda b, h, i: (h, b))],
        out_specs=pl.BlockSpec((1, ATT_TQ, ATT_HPS * V_HEAD_DIM), lambda b, h, i: (b, i, h)),
        out_shape=jax.ShapeDtypeStruct((B, SEQ, MLA_HEADS * V_HEAD_DIM), BF16),
        compiler_params=_cparams(("parallel", "parallel", "parallel")),
        name="attn",
    )(q3, k3, vt)


FILT_TL = 512


def _filter_kernel(z_ref, w1_ref, b1_ref, w2_ref, b2_ref, w3_ref, b3_ref, w4_ref, b4_ref, fr_ref,
                   dl_ref, o_ref):
    hp = functools.partial(jnp.dot, precision=lax.Precision.HIGHEST, preferred_element_type=F32)
    z = z_ref[...]
    fr = fr_ref[...]
    h = jnp.sin(fr * (hp(z, w1_ref[...]) + b1_ref[...]))
    h = jnp.sin(fr * (hp(h, w2_ref[...]) + b2_ref[...]))
    h = jnp.sin(fr * (hp(h, w3_ref[...]) + b3_ref[...]))
    h4 = hp(h, w4_ref[...]) + b4_ref[...]
    decay = jnp.exp(-z[:, 0:1] * dl_ref[...])
    hf = h4[:, :HYENA_WIDTH] * decay
    hb = h4[:, HYENA_WIDTH:] * decay
    row = pl.program_id(0) * FILT_TL + lax.broadcasted_iota(jnp.int32, hb.shape, 0)
    hb = jnp.where(row == 0, 0.0, hb)
    o_ref[:, :HYENA_WIDTH] = (hf + hb).astype(BF16)
    o_ref[:, HYENA_WIDTH:] = (hf - hb).astype(BF16)


def _pad2(a, r, c):
    return jnp.zeros((r, c), a.dtype).at[:a.shape[0], :a.shape[1]].set(a)


def _filter(w1, b1, w2, b2, w3, b3, w4, b4, freq):
    z, deltas = (jnp.asarray(t) for t in _filter_consts())
    P = LANES
    args = (z, _pad2(w1, P, P), _pad2(b1[None], 1, P), _pad2(w2, P, P), _pad2(b2[None], 1, P),
            _pad2(w3, P, P), _pad2(b3[None], 1, P), _pad2(w4, P, 2 * HYENA_WIDTH), b4[None],
            _pad2(freq[None], 1, P), deltas)
    full = lambda a: pl.BlockSpec(a.shape, lambda i: (0, 0))
    return pl.pallas_call(
        _filter_kernel,
        grid=(SEQ // FILT_TL,),
        in_specs=[pl.BlockSpec((FILT_TL, P), lambda i: (i, 0))] + [full(a) for a in args[1:]],
        out_specs=pl.BlockSpec((FILT_TL, 2 * HYENA_WIDTH), lambda i: (i, 0)),
        out_shape=jax.ShapeDtypeStruct((SEQ, 2 * HYENA_WIDTH), BF16),
        compiler_params=_cparams(("parallel",)),
        name="filt",
    )(*args)


SPEC_TR = 512


def _kspec_kernel(w_ref, h_ref, o_ref):
    r = _dot(w_ref[...], h_ref[...])
    row = pl.program_id(0) * SPEC_TR + lax.broadcasted_iota(jnp.int32, (SPEC_TR, HYENA_WIDTH), 0)
    o_ref[...] = jnp.where(row <= SEQ, r[:, :HYENA_WIDTH], r[:, HYENA_WIDTH:])


def _kspec(wf, hsd):
    return pl.pallas_call(
        _kspec_kernel,
        grid=(FFT_N // SPEC_TR,),
        in_specs=[pl.BlockSpec((SPEC_TR, SEQ), lambda i: (i, 0)),
                  pl.BlockSpec((SEQ, 2 * HYENA_WIDTH), lambda i: (0, 0))],
        out_specs=pl.BlockSpec((SPEC_TR, HYENA_WIDTH), lambda i: (i, 0)),
        out_shape=jax.ShapeDtypeStruct((FFT_N, HYENA_WIDTH), F32),
        compiler_params=_cparams(("parallel",)),
        name="kspec",
    )(wf, hsd)


def _short_conv(u, w, b):
    row = lax.broadcasted_iota(jnp.int32, u.shape, 0)
    prev = jnp.where(row == 0, 0.0, pltpu.roll(u, 1, 0))
    nxt = jnp.where(row == SEQ - 1, 0.0, pltpu.roll(u, SEQ - 1, 0))
    return prev * w[0:1] + u * w[1:2] + nxt * w[2:3] + b


def _hyconv_kernel(x0_ref, x1_ref, v_ref, cw_ref, cb_ref, bias_ref, z_ref, p1_ref, p2_ref):
    C = HYENA_WIDTH
    cw = cw_ref[...]
    cb = cb_ref[...]
    x0 = _short_conv(x0_ref[0].astype(F32), cw[:, 0:C], cb[:, 0:C])
    x1 = _short_conv(x1_ref[0].astype(F32), cw[:, C:2 * C], cb[:, C:2 * C])
    v = _short_conv(v_ref[0].astype(F32), cw[:, 2 * C:], cb[:, 2 * C:])
    z = v * x1
    z_ref[0] = z.astype(BF16)
    p1_ref[0] = x0.astype(BF16)
    p2_ref[0] = (x0 * z * bias_ref[...]).astype(BF16)


def _hyconv(proj3, cw, cb, bias):
    B = proj3.shape[0]
    cblk = lambda c: pl.BlockSpec((1, SEQ, HYENA_WIDTH), lambda b: (b, 0, c))
    full = lambda a: pl.BlockSpec(a.shape, lambda b: (0, 0))
    out = pl.BlockSpec((1, SEQ, HYENA_WIDTH), lambda b: (b, 0, 0))
    return pl.pallas_call(
        _hyconv_kernel,
        grid=(B,),
        in_specs=[cblk(COL_HY // HYENA_WIDTH), cblk(COL_HY // HYENA_WIDTH + 1), cblk(COL_HY // HYENA_WIDTH + 2),
                  full(cw), full(cb), full(bias)],
        out_specs=[out, out, out],
        out_shape=[jax.ShapeDtypeStruct((B, SEQ, HYENA_WIDTH), BF16)] * 3,
        compiler_params=_cparams(("parallel",)),
        name="hyconv",
    )(proj3, proj3, proj3, cw, cb, bias)


def _hyfwd_kernel(z_ref, wre_ref, wim_ref, kre_ref, kim_ref, y_ref):
    r = pl.program_id(0)
    z = z_ref[0]
    a = _dot(wre_ref[...], z)
    q = _dot(wim_ref[...], z)
    ka = kre_ref[...]
    kq = kim_ref[...]
    first = jnp.logical_and(r == 0, lax.broadcasted_iota(jnp.int32, a.shape, 0) == 0)
    yr = jnp.where(first, a * ka, a * ka - q * kq)
    yq = jnp.where(first, q * kq, a * kq + q * ka)
    sc = jnp.where(first, 1.0 / FFT_N, 2.0 / FFT_N)
    y_ref[0, 0] = (yr * sc).astype(BF16)
    y_ref[0, 1] = (yq * sc).astype(BF16)


HY_TR = 1024


def _hyfwd(z, wf, kp):
    B = z.shape[0]
    nr = SEQ // HY_TR
    y = pl.pallas_call(
        _hyfwd_kernel,
        grid=(nr, B),
        in_specs=[pl.BlockSpec((1, SEQ, HYENA_WIDTH), lambda r, b: (b, 0, 0)),
                  pl.BlockSpec((HY_TR, SEQ), lambda r, b: (r, 0)),
                  pl.BlockSpec((HY_TR, SEQ), lambda r, b: (r + nr, 0)),
                  pl.BlockSpec((HY_TR, HYENA_WIDTH), lambda r, b: (r, 0)),
                  pl.BlockSpec((HY_TR, HYENA_WIDTH), lambda r, b: (r + nr, 0))],
        out_specs=pl.BlockSpec((1, 2, HY_TR, HYENA_WIDTH), lambda r, b: (b, 0, r, 0)),
        out_shape=jax.ShapeDtypeStruct((B, 2, SEQ, HYENA_WIDTH), BF16),
        compiler_params=_cparams(("parallel", "parallel")),
        name="hyfwd",
    )(z, wf, wf, kp, kp)
    return y.reshape(B, FFT_N, HYENA_WIDTH)


HYI_TT = 1024


def _hyinv_kernel(wt_ref, y_ref, p1_ref, p2_ref, o_ref):
    conv = _dot(wt_ref[...], y_ref[0])
    o_ref[0] = (p1_ref[0].astype(F32) * conv + p2_ref[0].astype(F32)).astype(BF16)


def _hyinv(wft, ys, p1, p2):
    B = ys.shape[0]
    blk = pl.BlockSpec((1, HYI_TT, HYENA_WIDTH), lambda t, b: (b, t, 0))
    return pl.pallas_call(
        _hyinv_kernel,
        grid=(SEQ // HYI_TT, B),
        in_specs=[pl.BlockSpec((HYI_TT, FFT_N), lambda t, b: (t, 0)),
                  pl.BlockSpec((1, FFT_N, HYENA_WIDTH), lambda t, b: (b, 0, 0)),
                  blk, blk],
        out_specs=blk,
        out_shape=jax.ShapeDtypeStruct((B, SEQ, HYENA_WIDTH), BF16),
        compiler_params=_cparams(("parallel", "parallel")),
        name="hyinv",
    )(wft, ys, p1, p2)


MERGE_TM = 1024
QP_SLOTS = PEER_HEADS * 2


def _merge_kernel(att_ref, hy_ref, gl_ref, bg_ref, x_ref, woa_ref, woh_ref, wout_ref, fg_ref, wq_ref,
                  h_ref, hn_ref, qp_ref):
    a = _dot(att_ref[...], woa_ref[...])
    yh = _dot(hy_ref[...], woh_ref[...])
    g = 1.0 / (1.0 + jnp.exp(-(gl_ref[...].astype(F32) + bg_ref[...])))
    merged = (g[:, :D_MODEL] * a + g[:, D_MODEL:] * yh).astype(BF16)
    h = x_ref[...] + _dot(merged, wout_ref[...])
    h_ref[...] = h
    hn = _rms(h, fg_ref[...])
    hn_ref[...] = (hn * INV_SQRT2).astype(BF16)
    qp = _dot(hn.astype(BF16), wq_ref[...])
    for c in range(QP_SLOTS):
        qp_ref[c] = qp[:, c * LANES:(c + 1) * LANES].astype(BF16)


def _merge(att, hy, proj, bg, x2, woa, woh, wout, fg, wq):
    T = x2.shape[0]
    half = PEER_KEY_DIM // 2
    row = lambda w: pl.BlockSpec((MERGE_TM, w), lambda i: (i, 0))
    full = lambda a: pl.BlockSpec(a.shape, lambda i: (0, 0))
    return pl.pallas_call(
        _merge_kernel,
        grid=(T // MERGE_TM,),
        in_specs=[row(att.shape[1]), row(hy.shape[1]),
                  pl.BlockSpec((MERGE_TM, 2 * D_MODEL), lambda i: (i, COL_GATE // (2 * D_MODEL))),
                  full(bg), row(D_MODEL), full(woa), full(woh), full(wout), full(fg), full(wq)],
        out_specs=[row(D_MODEL), row(D_MODEL),
                   pl.BlockSpec((QP_SLOTS, MERGE_TM, half), lambda i: (0, i, 0))],
        out_shape=[jax.ShapeDtypeStruct((T, D_MODEL), F32),
                   jax.ShapeDtypeStruct((T, D_MODEL), BF16),
                   jax.ShapeDtypeStruct((QP_SLOTS, T, half), BF16)],
        compiler_params=_cparams(("parallel",)),
        name="merge",
    )(att, hy, proj, bg, x2, woa, woh, wout, fg, wq)


ROUTE_TB = LANES
NEG_INF = float("-inf")
N_CHAINS = 2 * PEER_HEADS


def _sort_pairs(n):
    pairs = []

    def merge(lo, hi, r):
        step = 2 * r
        if step < hi - lo:
            merge(lo, hi, step)
            merge(lo + r, hi, step)
            pairs.extend((i, i + r) for i in range(lo + r, hi - r, step))
        else:
            pairs.append((lo, lo + r))

    def sort(lo, hi):
        if hi > lo:
            mid = lo + (hi - lo) // 2
            sort(lo, mid)
            sort(mid + 1, hi)
            merge(lo, hi, 1)
    sort(0, n - 1)
    return pairs


SUBLANES = 8


def _top_sorted(s):
    n = s.shape[0] // SUBLANES
    v = [s[SUBLANES * i:SUBLANES * (i + 1), :] for i in range(n)]
    for i, j in _sort_pairs(n):
        v[i], v[j] = jnp.maximum(v[i], v[j]), jnp.minimum(v[i], v[j])
    out = []
    for k in range(PEER_TOPK):
        m = jnp.max(v[0], axis=0, keepdims=True)
        out.append(m)
        if k + 1 < PEER_TOPK:
            popped = v[0] == m
            for d in range(PEER_TOPK - 1 - k):
                v[d] = jnp.where(popped, v[d + 1], v[d])
    return jnp.concatenate(out, axis=0)


CAND_ROWS = 72


def _extract_top(work_ref, vals_ref, n_chains):
    slot = lax.broadcasted_iota(jnp.int32, vals_ref.shape[1:], 0)

    def body(k, carry):
        for c in range(n_chains):
            s = work_ref[c]
            m = jnp.max(s, axis=0, keepdims=True)
            work_ref[c] = jnp.where(s == m, NEG_INF, s)
            vals_ref[c] = jnp.where(slot == k, m, vals_ref[c])
        return carry
    lax.fori_loop(0, PEER_TOPK, body, 0)


def _candidates(v1, v2):
    a16 = lax.broadcasted_iota(jnp.int32, v1.shape, 0)
    a8 = a16[0:8]
    return jnp.concatenate([
        v1[0:1] + v2,
        v1[1:2] + v2[0:8], v1[2:3] + v2[0:8], v1[3:4] + v2[0:8],
        jnp.where(a16 >= 4, v1 + v2[0:1], NEG_INF),
        jnp.where(a8 >= 4, v1[0:8] + v2[1:2], NEG_INF),
        jnp.where(a8 == 4, v1[0:8] + v2[2:3], NEG_INF),
    ], axis=0)


def _route_kernel(q_ref, k1_ref, k2_ref, cnt_ref, a_ref, rank_ref, b_ref,
                  s_ref, vals_ref, cand_ref, cvals_ref):
    for h in range(PEER_HEADS):
        for side, k_ref in enumerate((k1_ref, k2_ref)):
            s = _dot_nt(k_ref[h], q_ref[2 * h + side])
            s_ref[2 * h + side] = s
            vals_ref[2 * h + side] = _top_sorted(s)
    for h in range(PEER_HEADS):
        cand_ref[h] = _candidates(vals_ref[2 * h], vals_ref[2 * h + 1])
    cvals_ref[...] = jnp.zeros_like(cvals_ref)
    _extract_top(cand_ref, cvals_ref, PEER_HEADS)

    for h in range(PEER_HEADS):
        v1, v2, cv = vals_ref[2 * h], vals_ref[2 * h + 1], cvals_ref[h]
        s1, s2 = s_ref[2 * h], s_ref[2 * h + 1]
        tau = cv[PEER_TOPK - 1:PEER_TOPK]
        zsum = jnp.sum(jnp.exp(cv - cv[0:1]), axis=0, keepdims=True)
        cnt_sorted = jnp.zeros_like(v1)
        for b in range(PEER_TOPK):
            cnt_sorted = cnt_sorted + jnp.where(v1 + v2[b:b + 1] >= tau, 1.0, 0.0)
        cnt = jnp.zeros_like(s1)
        for a in range(PEER_TOPK):
            cnt = jnp.where(s1 == v1[a:a + 1], cnt_sorted[a:a + 1], cnt)
        rank = jnp.full_like(s2, float(PEER_TOPK))
        for b in reversed(range(PEER_TOPK)):
            rank = jnp.where(s2 >= v2[b:b + 1], float(b), rank)
        cnt_ref[0, h] = cnt * RANK_SCALE
        a_ref[0, h] = jnp.exp(s1 - v1[0:1]) * (GELU_SCALE / zsum)
        rank_ref[0, h] = (rank * RANK_SCALE).astype(BF16)
        b_ref[0, h] = jnp.exp(s2 - v2[0:1]).astype(BF16)


def _route(qp, k1, k2):
    T = qp.shape[1]
    half = PEER_KEY_DIM // 2
    blk = pl.BlockSpec((1, PEER_HEADS, N_KEYS, LANES), lambda i: (i, 0, 0, 0))
    shape = lambda dt: jax.ShapeDtypeStruct((T // LANES, PEER_HEADS, N_KEYS, LANES), dt)
    keys = pl.BlockSpec((PEER_HEADS, N_KEYS, half), lambda i: (0, 0, 0))
    return pl.pallas_call(
        _route_kernel,
        grid=(T // ROUTE_TB,),
        in_specs=[pl.BlockSpec((QP_SLOTS, ROUTE_TB, half), lambda i: (0, i, 0)), keys, keys],
        out_specs=[blk, blk, blk, blk],
        out_shape=[shape(F32), shape(F32), shape(BF16), shape(BF16)],
        scratch_shapes=[pltpu.VMEM((N_CHAINS, N_KEYS, ROUTE_TB), F32),
                        pltpu.VMEM((N_CHAINS, PEER_TOPK, ROUTE_TB), F32),
                        pltpu.VMEM((PEER_HEADS, CAND_ROWS, ROUTE_TB), F32),
                        pltpu.VMEM((PEER_HEADS, PEER_TOPK, ROUTE_TB), F32)],
        compiler_params=_cparams(("parallel",)),
        name="route",
    )(qp, k1, k2)


PEER_TB = 512
PEER_TE = 2048
PEER_ROWS = PEER_TE // N_KEYS
PEER_NC = PEER_TB // LANES
INV_SQRT2 = 1.0 / math.sqrt(2.0)
GELU_SCALE = 0.5 * math.sqrt(2.0)
RANK_SCALE = 256.0


def _row_tile(ref, c, h, ii):
    return jnp.broadcast_to(ref[c, h, ii:ii + 1, :], (N_KEYS, LANES)).astype(BF16)


def _peer_kernel(hn_ref, u_ref, vt_ref, cnt_ref, a_ref, rank_ref, b_ref, h_ref, o_ref, y_ref, act_ref, wt_ref):
    e = pl.program_id(1)

    @pl.when(e == 0)
    def _():
        y_ref[...] = jnp.zeros_like(y_ref)

    act = _dot_nt(u_ref[...], hn_ref[...])
    for c in range(PEER_NC):
        act_ref[c] = act[:, c * LANES:(c + 1) * LANES].astype(BF16)

    def chunk(c, carry):
        for ii in range(PEER_ROWS):
            rs = slice(ii * N_KEYS, (ii + 1) * N_KEYS)
            w = jnp.zeros((N_KEYS, LANES), BF16)
            for h in range(PEER_HEADS):
                cnt = _row_tile(cnt_ref, c, h, ii)
                wa = _row_tile(a_ref, c, h, ii)
                sel = jnp.minimum(jnp.maximum(cnt - rank_ref[c, h], 0), b_ref[c, h])
                w = w + sel * wa
            x = act_ref[c, rs, :]
            wt_ref[c, rs, :] = x * (1 + lax.erf(x)) * w
        return carry
    lax.fori_loop(0, PEER_NC, chunk, 0)

    wt = jnp.concatenate([wt_ref[c] for c in range(PEER_NC)], axis=1)
    y_ref[...] += _dot(vt_ref[...], wt)

    @pl.when(e == pl.num_programs(1) - 1)
    def _():
        o_ref[...] = h_ref[...] + y_ref[...].T


def _peer(hn, u, vt, cnt, a, rank, b, h):
    T = hn.shape[0]
    rows = pl.BlockSpec((PEER_NC, PEER_HEADS, PEER_ROWS, LANES), lambda t, e: (t, 0, e, 0))
    keys = pl.BlockSpec((PEER_NC, PEER_HEADS, N_KEYS, LANES), lambda t, e: (t, 0, 0, 0))
    tok = pl.BlockSpec((PEER_TB, D_MODEL), lambda t, e: (t, 0))
    return pl.pallas_call(
        _peer_kernel,
        grid=(T // PEER_TB, N_EXPERTS // PEER_TE),
        in_specs=[tok,
                  pl.BlockSpec((PEER_TE, D_MODEL), lambda t, e: (e, 0)),
                  pl.BlockSpec((D_MODEL, PEER_TE), lambda t, e: (0, e)),
                  rows, rows, keys, keys, tok],
        out_specs=tok,
        out_shape=jax.ShapeDtypeStruct((T, D_MODEL), F32),
        scratch_shapes=[pltpu.VMEM((D_MODEL, PEER_TB), F32),
                        pltpu.VMEM((PEER_NC, PEER_TE, LANES), BF16),
                        pltpu.VMEM((PEER_NC, PEER_TE, LANES), BF16)],
        compiler_params=_cparams(("parallel", "arbitrary")),
        name="peer",
    )(hn, u, vt, cnt, a, rank, b, h)


def _regroup_w_in(w):
    o_kv = Q_LORA_RANK
    o_pe = o_kv + KV_LORA_RANK
    o_hy = o_pe + QK_ROPE_DIM
    o_gate = o_hy + 3 * HYENA_WIDTH
    out = jnp.zeros((D_MODEL, PROJ_COLS), BF16)
    out = out.at[:, COL_CQ:COL_CQ + Q_LORA_RANK].set(w[:, :o_kv].astype(BF16))
    out = out.at[:, COL_CKV:COL_CKV + KV_LORA_RANK].set(w[:, o_kv:o_pe].astype(BF16))
    out = out.at[:, COL_KPE + KPE_LANE:COL_KPE + KPE_LANE + QK_ROPE_DIM].set(w[:, o_pe:o_hy].astype(BF16))
    out = out.at[:, COL_HY:COL_HY + 3 * HYENA_WIDTH].set(w[:, o_hy:o_gate].astype(BF16))
    out = out.at[:, COL_GATE:].set(w[:, o_gate:].astype(BF16))
    return out


def _head_slots(w, width):
    k = w.shape[0]
    w3 = w.reshape(k, MLA_HEADS, width).astype(BF16)
    return jnp.zeros((k, MLA_HEADS, HEAD_SLOT), BF16).at[:, :, :width].set(w3).reshape(k, HW)


def kernel(x, attn_norm, w_in, b_gate, q_a_norm, w_uq, kv_a_norm, w_ukv, q_norm, k_norm, w_o_attn, hyena_conv_w, hyena_conv_b, filt_w1, filt_b1, filt_w2, filt_b2, filt_w3, filt_b3, filt_w4, filt_b4, filt_freq, hyena_bias, w_o_hyena, w_out, ffn_norm, peer_w_q, peer_keys1, peer_keys2, expert_u, expert_v):
    B = x.shape[0]
    T = B * SEQ
    x2 = x.reshape(T, D_MODEL)
    bf = lambda a: a.astype(BF16)

    proj = _inproj(x2, attn_norm, _regroup_w_in(w_in[0]))
    kv3 = w_ukv[0].reshape(KV_LORA_RANK, MLA_HEADS, QK_NOPE_DIM + V_HEAD_DIM)
    wuk_p = _head_slots(kv3[:, :, :QK_NOPE_DIM].reshape(KV_LORA_RANK, -1), QK_NOPE_DIM)
    wuv_p = _head_slots(kv3[:, :, QK_NOPE_DIM:].reshape(KV_LORA_RANK, -1), V_HEAD_DIM)
    vone = jnp.zeros((MLA_HEADS, HEAD_SLOT), BF16).at[:, V_HEAD_DIM].set(1.0).reshape(HW, 1)
    wuvt_p = jnp.concatenate([wuv_p.T, _pad2(vone, HW, KV_LORA_RANK)], axis=1)
    wuq_p = _head_slots(w_uq[0], QK_HEAD_DIM)
    wuqr_p = _rope_partner(wuq_p.reshape(Q_LORA_RANK, MLA_HEADS, HEAD_SLOT), axis=2).reshape(Q_LORA_RANK, HW)
    gq = _pad2(q_norm, 1, HEAD_SLOT)
    q, k, vt = _mlaprep(proj, q_a_norm, kv_a_norm, wuq_p, wuqr_p, wuk_p, wuvt_p,
                        gq, _rope_partner(gq, axis=1), _pad2(k_norm, 1, HEAD_SLOT))
    att = _attention(q, k, vt, B).reshape(T, MLA_HEADS * V_HEAD_DIM)

    wf, wft = (jnp.asarray(w).astype(BF16) for w in _dft_tables())
    hsd = _filter(filt_w1[0], filt_b1[0], filt_w2[0], filt_b2[0], filt_w3[0], filt_b3[0], filt_w4[0],
                  filt_b4[0], filt_freq[0])
    kp = _kspec(wf, hsd)
    z, p1, p2 = _hyconv(proj.reshape(B, SEQ, PROJ_COLS), hyena_conv_w[0], hyena_conv_b, hyena_bias)
    ys = _hyfwd(z, wf, kp)
    hy = _hyinv(wft, ys, p1, p2).reshape(T, HYENA_WIDTH)

    h, hn, qp = _merge(att, hy, proj, b_gate, x2, bf(w_o_attn[0]), bf(w_o_hyena[0]), bf(w_out[0]),
                       ffn_norm, bf(peer_w_q[0]))

    cnt, a, rank, b = _route(qp, bf(peer_keys1[0]), bf(peer_keys2[0]))
    out = _peer(hn, bf(expert_u[0]), bf(expert_v[0]).T, cnt, a, rank, b, h)
    return out.reshape(B, SEQ, D_MODEL)
```

```python
import functools
import math

import numpy as np
import jax
import jax.numpy as jnp
from jax import lax
from jax.experimental import pallas as pl
from jax.experimental.pallas import tpu as pltpu

F32 = jnp.float32
BF16 = jnp.bfloat16
FP8 = jnp.float8_e4m3fn
V_SCALE = 8.0
WT_SCALE = 64.0

D_MODEL = 1024
SEQ = 2048
MLA_HEADS = 8
QK_NOPE_DIM = 64
QK_ROPE_DIM = 32
QK_HEAD_DIM = QK_NOPE_DIM + QK_ROPE_DIM
V_HEAD_DIM = 64
Q_LORA_RANK = 256
KV_LORA_RANK = 128
ROPE_THETA = 10000.0
HYENA_WIDTH = 512
FILTER_EMB = 33
FILTER_ORDER = 64
FAST_DECAY_PCT = 0.3
SLOW_DECAY_PCT = 1.5
DECAY_TARGET = 1e-2
PEER_HEADS = 8
N_KEYS = 128
N_EXPERTS = N_KEYS * N_KEYS
PEER_KEY_DIM = 256
PEER_TOPK = 16
EPS = 1e-6

LANES = 128
HEAD_SLOT = LANES
FFT_N = 2 * SEQ
PROJ_COLS = 4096
COL_CQ, COL_CKV, COL_KPE, COL_HY, COL_GATE = 0, 256, 384, 512, 2048
KPE_LANE = QK_NOPE_DIM
VMEM_LIMIT = 56 * 1024 * 1024


def _cparams(sem):
    return pltpu.CompilerParams(dimension_semantics=sem, vmem_limit_bytes=VMEM_LIMIT)


def _dot(a, b):
    return jnp.dot(a, b, preferred_element_type=F32)


def _dot_nt(a, b):
    return lax.dot_general(a, b, (((1,), (1,)), ((), ())), preferred_element_type=F32)


def _rms(x, g):
    return x * lax.rsqrt(jnp.mean(x * x, axis=-1, keepdims=True) + EPS) * g


@functools.lru_cache(maxsize=None)
def _dft_tables():
    p = np.arange(FFT_N, dtype=np.int64)[:, None]
    s = np.arange(SEQ, dtype=np.int64)[None, :]
    f = np.where(p <= SEQ, p, p - SEQ)
    ang = (2.0 * np.pi / FFT_N) * ((f * s) % FFT_N).astype(np.float64)
    w = np.where(p <= SEQ, np.cos(ang), np.sin(ang)).astype(np.float32)
    return w, np.ascontiguousarray(w.T)


@functools.lru_cache(maxsize=None)
def _rotary_tables():
    half = QK_ROPE_DIM // 2
    inv_freq = ROPE_THETA ** (-np.arange(half, dtype=np.float64) / half)
    ang = np.arange(SEQ, dtype=np.float64)[:, None] * inv_freq[None, :]
    cos = np.ones((SEQ, HEAD_SLOT)); sa = np.zeros((SEQ, HEAD_SLOT)); sb = np.zeros((SEQ, HEAD_SLOT))
    lo, mid, hi = QK_NOPE_DIM, QK_NOPE_DIM + half, QK_HEAD_DIM
    cos[:, lo:mid] = np.cos(ang); cos[:, mid:hi] = np.cos(ang)
    sb[:, lo:mid] = -np.sin(ang)
    sa[:, mid:hi] = np.sin(ang)
    return tuple(t.astype(np.float32) for t in (cos, sa, sb))


@functools.lru_cache(maxsize=None)
def _filter_consts():
    L = SEQ
    t = np.linspace(0.0, 1.0, L)[:, None]
    bands = (FILTER_EMB - 1) // 2
    w = 2.0 * math.pi * np.arange(L)[:, None] / L
    f = np.linspace(1e-4, bands - 1, bands)[None, :]
    z = np.concatenate([t, np.cos(f * w), -np.sin(f * w)], axis=-1)
    zp = np.zeros((L, LANES)); zp[:, :FILTER_EMB] = z
    min_decay = math.log(DECAY_TARGET) / SLOW_DECAY_PCT
    max_decay = math.log(DECAY_TARGET) / FAST_DECAY_PCT
    deltas = np.abs(np.linspace(min_decay, max_decay, HYENA_WIDTH))[None, :]
    return zp.astype(np.float32), deltas.astype(np.float32)


INPROJ_TM = 1024
INPROJ_CW = 1024


def _inproj_kernel(x_ref, g_ref, w_ref, o_ref):
    xn = _rms(x_ref[...], g_ref[...]).astype(BF16)
    for c in range(PROJ_COLS // INPROJ_CW):
        sl = slice(c * INPROJ_CW, (c + 1) * INPROJ_CW)
        o_ref[:, sl] = _dot(xn, w_ref[:, sl]).astype(BF16)


def _inproj(x2, g, w_p):
    T = x2.shape[0]
    return pl.pallas_call(
        _inproj_kernel,
        grid=(T // INPROJ_TM,),
        in_specs=[pl.BlockSpec((INPROJ_TM, D_MODEL), lambda i: (i, 0)),
                  pl.BlockSpec((1, D_MODEL), lambda i: (0, 0)),
                  pl.BlockSpec((D_MODEL, PROJ_COLS), lambda i: (0, 0))],
        out_specs=pl.BlockSpec((INPROJ_TM, PROJ_COLS), lambda i: (i, 0)),
        out_shape=jax.ShapeDtypeStruct((T, PROJ_COLS), BF16),
        compiler_params=_cparams(("parallel",)),
        name="inproj",
    )(x2, g, w_p)


MLA_TM = 1024
HW = MLA_HEADS * HEAD_SLOT


def _rope_partner(a, axis):
    half = QK_ROPE_DIM // 2
    lo, mid, hi = QK_NOPE_DIM, QK_NOPE_DIM + half, QK_HEAD_DIM
    take = lambda s, e: lax.slice_in_dim(a, s, e, axis=axis)
    pad = [(0, 0)] * a.ndim
    pad[axis] = (lo, a.shape[axis] - hi)
    return jnp.pad(jnp.concatenate([take(mid, hi), take(lo, mid)], axis=axis), pad)


def _mlaprep_kernel(p_ref, qa_ref, kva_ref, wuq_ref, wuqr_ref, wuk_ref, wuvt_ref, gq_ref, gqr_ref, gk_ref,
                    ones_ref, cos_ref, sa_ref, sb_ref, q_ref, k_ref, vt_ref):
    p = p_ref[...].astype(F32)
    cqn = _rms(p[:, COL_CQ:COL_CQ + Q_LORA_RANK], qa_ref[...]).astype(BF16)
    ckvn = _rms(p[:, COL_CKV:COL_CKV + KV_LORA_RANK], kva_ref[...]).astype(BF16)
    kpe = p[:, COL_KPE:COL_KPE + HEAD_SLOT]
    q = _dot(cqn, wuq_ref[...])
    qr = _dot(cqn, wuqr_ref[...])
    kn = _dot(ckvn, wuk_ref[...])
    lane = lax.broadcasted_iota(jnp.int32, ckvn.shape, 1)
    ckv_one = jnp.concatenate([ckvn, jnp.where(lane == 0, 1.0, 0.0).astype(BF16)], axis=1)
    vt_ref[...] = _dot_nt(wuvt_ref[...], ckv_one).astype(BF16)
    ones = ones_ref[...]
    pair = 2 * HEAD_SLOT

    def head_sums(x):
        sq = (x * x).astype(BF16)
        return jnp.concatenate([_dot(sq[:, i:i + pair], ones) for i in range(0, HW, pair)], axis=1)
    ssq = head_sums(q)
    ssk = head_sums(kn)
    sspe = _dot((kpe * kpe).astype(BF16), ones_ref[:HEAD_SLOT, :HEAD_SLOT])
    cos, sa, sb = cos_ref[...], sa_ref[...], sb_ref[...]
    scale = QK_HEAD_DIM ** -0.5 * math.log2(math.e)
    inv_d = 1.0 / QK_HEAD_DIM
    tq = cos * (gq_ref[...] * scale)
    tqr = (sa + sb) * (gqr_ref[...] * scale)
    tk = cos * gk_ref[...]
    kg = kpe * gk_ref[...]
    half = QK_ROPE_DIM // 2
    rk = pltpu.roll(kg, half, 1) * sa + pltpu.roll(kg, HEAD_SLOT - half, 1) * sb
    for h in range(MLA_HEADS):
        sl = slice(h * HEAD_SLOT, (h + 1) * HEAD_SLOT)
        inv_q = lax.rsqrt(ssq[:, sl] * inv_d + EPS)
        q_ref[:, sl] = ((q[:, sl] * tq + qr[:, sl] * tqr) * inv_q).astype(BF16)
        inv_k = lax.rsqrt((ssk[:, sl] + sspe) * inv_d + EPS)
        k_ref[:, sl] = (((kn[:, sl] + kpe) * tk + rk) * inv_k).astype(BF16)


def _mlaprep(proj, qa, kva, wuq_p, wuqr_p, wuk_p, wuvt_p, gq, gqr, gk):
    T = proj.shape[0]
    cos, sa, sb = (jnp.asarray(t) for t in _rotary_tables())
    head_of = np.arange(2 * HEAD_SLOT) // HEAD_SLOT
    ones = jnp.asarray(head_of[:, None] == head_of[None, :], dtype=BF16)
    nseq = SEQ // MLA_TM
    full = lambda shape: pl.BlockSpec(shape, lambda i: (0, 0))
    pos = pl.BlockSpec((MLA_TM, HEAD_SLOT), lambda i: (i % nseq, 0))
    out = pl.BlockSpec((MLA_TM, HW), lambda i: (i, 0))
    return pl.pallas_call(
        _mlaprep_kernel,
        grid=(T // MLA_TM,),
        in_specs=[pl.BlockSpec((MLA_TM, 512), lambda i: (i, 0)),
                  full((1, Q_LORA_RANK)), full((1, KV_LORA_RANK)),
                  full((Q_LORA_RANK, HW)), full((Q_LORA_RANK, HW)), full((KV_LORA_RANK, HW)),
                  full((HW, 2 * KV_LORA_RANK)),
                  full((1, HEAD_SLOT)), full((1, HEAD_SLOT)), full((1, HEAD_SLOT)),
                  full((2 * HEAD_SLOT, 2 * HEAD_SLOT)), pos, pos, pos],
        out_specs=[out, out, pl.BlockSpec((HW, MLA_TM), lambda i: (0, i))],
        out_shape=[jax.ShapeDtypeStruct((T, HW), BF16)] * 2 + [jax.ShapeDtypeStruct((HW, T), BF16)],
        compiler_params=_cparams(("parallel",)),
        name="mlaprep",
    )(proj, qa, kva, wuq_p, wuqr_p, wuk_p, wuvt_p, gq, gqr, gk, ones, cos, sa, sb)


ATT_TQ = SEQ
ATT_HPS = 4


def _attn_kernel(q_ref, k_ref, vt_ref, o_ref):
    outs = []
    for hh in range(ATT_HPS):
        sl = slice(hh * HEAD_SLOT, (hh + 1) * HEAD_SLOT)
        st = _dot_nt(k_ref[0, :, sl], q_ref[0, :, sl])
        sb = st.astype(BF16)
        p = jnp.exp2(sb - jnp.max(sb, axis=0, keepdims=True))
        ot = _dot(vt_ref[sl, :], p)
        on = ot[:V_HEAD_DIM, :] * (1.0 / ot[V_HEAD_DIM:V_HEAD_DIM + 1, :])
        outs.append(on.T)
    o_ref[0] = jnp.concatenate(outs, axis=-1).astype(BF16)


def _attention(q, k, vt, B):
    q3, k3 = (a.reshape(B, SEQ, HW) for a in (q, k))
    pair = ATT_HPS * HEAD_SLOT
    return pl.pallas_call(
        _attn_kernel,
        grid=(B, MLA_HEADS // ATT_HPS, SEQ // ATT_TQ),
        in_specs=[pl.BlockSpec((1, ATT_TQ, pair), lambda b, h, i: (b, i, h)),
                  pl.BlockSpec((1, SEQ, pair), lambda b, h, i: (b, 0, h)),
                  pl.BlockSpec((pair, SEQ), lambda b, h, i: (h, b))],
        out_specs=pl.BlockSpec((1, ATT_TQ, ATT_HPS * V_HEAD_DIM), lambda b, h, i: (b, i, h)),
        out_shape=jax.ShapeDtypeStruct((B, SEQ, MLA_HEADS * V_HEAD_DIM), BF16),
        compiler_params=_cparams(("parallel", "parallel", "parallel")),
        name="attn",
    )(q3, k3, vt)


FILT_TL = 512


def _filter_kernel(z_ref, w1_ref, b1_ref, w2_ref, b2_ref, w3_ref, b3_ref, w4_ref, b4_ref, fr_ref,
                   dl_ref, o_ref):
    hp = functools.partial(jnp.dot, precision=lax.Precision.HIGHEST, preferred_element_type=F32)
    z = z_ref[...]
    fr = fr_ref[...]
    h = jnp.sin(fr * (hp(z, w1_ref[...]) + b1_ref[...]))
    h = jnp.sin(fr * (hp(h, w2_ref[...]) + b2_ref[...]))
    h = jnp.sin(fr * (hp(h, w3_ref[...]) + b3_ref[...]))
    h4 = hp(h, w4_ref[...]) + b4_ref[...]
    decay = jnp.exp(-z[:, 0:1] * dl_ref[...])
    hf = h4[:, :HYENA_WIDTH] * decay
    hb = h4[:, HYENA_WIDTH:] * decay
    row = pl.program_id(0) * FILT_TL + lax.broadcasted_iota(jnp.int32, hb.shape, 0)
    hb = jnp.where(row == 0, 0.0, hb)
    o_ref[:, :HYENA_WIDTH] = (hf + hb).astype(BF16)
    o_ref[:, HYENA_WIDTH:] = (hf - hb).astype(BF16)


def _pad2(a, r, c):
    return jnp.zeros((r, c), a.dtype).at[:a.shape[0], :a.shape[1]].set(a)


def _filter(w1, b1, w2, b2, w3, b3, w4, b4, freq):
    z, deltas = (jnp.asarray(t) for t in _filter_consts())
    P = LANES
    args = (z, _pad2(w1, P, P), _pad2(b1[None], 1, P), _pad2(w2, P, P), _pad2(b2[None], 1, P),
            _pad2(w3, P, P), _pad2(b3[None], 1, P), _pad2(w4, P, 2 * HYENA_WIDTH), b4[None],
            _pad2(freq[None], 1, P), deltas)
    full = lambda a: pl.BlockSpec(a.shape, lambda i: (0, 0))
    return pl.pallas_call(
        _filter_kernel,
        grid=(SEQ // FILT_TL,),
        in_specs=[pl.BlockSpec((FILT_TL, P), lambda i: (i, 0))] + [full(a) for a in args[1:]],
        out_specs=pl.BlockSpec((FILT_TL, 2 * HYENA_WIDTH), lambda i: (i, 0)),
        out_shape=jax.ShapeDtypeStruct((SEQ, 2 * HYENA_WIDTH), BF16),
        compiler_params=_cparams(("parallel",)),
        name="filt",
    )(*args)


SPEC_TR = 512


def _kspec_kernel(w_ref, h_ref, o_ref):
    r = _dot(w_ref[...], h_ref[...])
    row = pl.program_id(0) * SPEC_TR + lax.broadcasted_iota(jnp.int32, (SPEC_TR, HYENA_WIDTH), 0)
    o_ref[...] = jnp.where(row <= SEQ, r[:, :HYENA_WIDTH], r[:, HYENA_WIDTH:])


def _kspec(wf, hsd):
    return pl.pallas_call(
        _kspec_kernel,
        grid=(FFT_N // SPEC_TR,),
        in_specs=[pl.BlockSpec((SPEC_TR, SEQ), lambda i: (i, 0)),
                  pl.BlockSpec((SEQ, 2 * HYENA_WIDTH), lambda i: (0, 0))],
        out_specs=pl.BlockSpec((SPEC_TR, HYENA_WIDTH), lambda i: (i, 0)),
        out_shape=jax.ShapeDtypeStruct((FFT_N, HYENA_WIDTH), F32),
        compiler_params=_cparams(("parallel",)),
        name="kspec",
    )(wf, hsd)


def _short_conv(u, w, b):
    row = lax.broadcasted_iota(jnp.int32, u.shape, 0)
    prev = jnp.where(row == 0, 0.0, pltpu.roll(u, 1, 0))
    nxt = jnp.where(row == SEQ - 1, 0.0, pltpu.roll(u, SEQ - 1, 0))
    return prev * w[0:1] + u * w[1:2] + nxt * w[2:3] + b


def _hyconv_kernel(x0_ref, x1_ref, v_ref, cw_ref, cb_ref, bias_ref, z_ref, p1_ref, p2_ref):
    C = HYENA_WIDTH
    cw = cw_ref[...]
    cb = cb_ref[...]
    x0 = _short_conv(x0_ref[0].astype(F32), cw[:, 0:C], cb[:, 0:C])
    x1 = _short_conv(x1_ref[0].astype(F32), cw[:, C:2 * C], cb[:, C:2 * C])
    v = _short_conv(v_ref[0].astype(F32), cw[:, 2 * C:], cb[:, 2 * C:])
    z = v * x1
    z_ref[0] = z.astype(BF16)
    p1_ref[0] = x0.astype(BF16)
    p2_ref[0] = (x0 * z * bias_ref[...]).astype(BF16)


def _hyconv(proj3, cw, cb, bias):
    B = proj3.shape[0]
    cblk = lambda c: pl.BlockSpec((1, SEQ, HYENA_WIDTH), lambda b: (b, 0, c))
    full = lambda a: pl.BlockSpec(a.shape, lambda b: (0, 0))
    out = pl.BlockSpec((1, SEQ, HYENA_WIDTH), lambda b: (b, 0, 0))
    return pl.pallas_call(
        _hyconv_kernel,
        grid=(B,),
        in_specs=[cblk(COL_HY // HYENA_WIDTH), cblk(COL_HY // HYENA_WIDTH + 1), cblk(COL_HY // HYENA_WIDTH + 2),
                  full(cw), full(cb), full(bias)],
        out_specs=[out, out, out],
        out_shape=[jax.ShapeDtypeStruct((B, SEQ, HYENA_WIDTH), BF16)] * 3,
        compiler_params=_cparams(("parallel",)),
        name="hyconv",
    )(proj3, proj3, proj3, cw, cb, bias)


def _hyfwd_kernel(z_ref, wre_ref, wim_ref, kre_ref, kim_ref, y_ref):
    r = pl.program_id(0)
    z = z_ref[0]
    a = _dot(wre_ref[...], z)
    q = _dot(wim_ref[...], z)
    ka = kre_ref[...]
    kq = kim_ref[...]
    first = jnp.logical_and(r == 0, lax.broadcasted_iota(jnp.int32, a.shape, 0) == 0)
    yr = jnp.where(first, a * ka, a * ka - q * kq)
    yq = jnp.where(first, q * kq, a * kq + q * ka)
    sc = jnp.where(first, 1.0 / FFT_N, 2.0 / FFT_N)
    y_ref[0, 0] = (yr * sc).astype(BF16)
    y_ref[0, 1] = (yq * sc).astype(BF16)


HY_TR = 1024


def _hyfwd(z, wf, kp):
    B = z.shape[0]
    nr = SEQ // HY_TR
    y = pl.pallas_call(
        _hyfwd_kernel,
        grid=(nr, B),
        in_specs=[pl.BlockSpec((1, SEQ, HYENA_WIDTH), lambda r, b: (b, 0, 0)),
                  pl.BlockSpec((HY_TR, SEQ), lambda r, b: (r, 0)),
                  pl.BlockSpec((HY_TR, SEQ), lambda r, b: (r + nr, 0)),
                  pl.BlockSpec((HY_TR, HYENA_WIDTH), lambda r, b: (r, 0)),
                  pl.BlockSpec((HY_TR, HYENA_WIDTH), lambda r, b: (r + nr, 0))],
        out_specs=pl.BlockSpec((1, 2, HY_TR, HYENA_WIDTH), lambda r, b: (b, 0, r, 0)),
        out_shape=jax.ShapeDtypeStruct((B, 2, SEQ, HYENA_WIDTH), BF16),
        compiler_params=_cparams(("parallel", "parallel")),
        name="hyfwd",
    )(z, wf, wf, kp, kp)
    return y.reshape(B, FFT_N, HYENA_WIDTH)


HYI_TT = 1024


def _hyinv_kernel(wt_ref, y_ref, p1_ref, p2_ref, o_ref):
    conv = _dot(wt_ref[...], y_ref[0])
    o_ref[0] = (p1_ref[0].astype(F32) * conv + p2_ref[0].astype(F32)).astype(BF16)


def _hyinv(wft, ys, p1, p2):
    B = ys.shape[0]
    blk = pl.BlockSpec((1, HYI_TT, HYENA_WIDTH), lambda t, b: (b, t, 0))
    return pl.pallas_call(
        _hyinv_kernel,
        grid=(SEQ // HYI_TT, B),
        in_specs=[pl.BlockSpec((HYI_TT, FFT_N), lambda t, b: (t, 0)),
                  pl.BlockSpec((1, FFT_N, HYENA_WIDTH), lambda t, b: (b, 0, 0)),
                  blk, blk],
        out_specs=blk,
        out_shape=jax.ShapeDtypeStruct((B, SEQ, HYENA_WIDTH), BF16),
        compiler_params=_cparams(("parallel", "parallel")),
        name="hyinv",
    )(wft, ys, p1, p2)


MERGE_TM = 1024
QP_SLOTS = PEER_HEADS * 2


def _merge_kernel(att_ref, hy_ref, gl_ref, bg_ref, x_ref, woa_ref, woh_ref, wout_ref, fg_ref, wq_ref,
                  h_ref, hn_ref, qp_ref):
    a = _dot(att_ref[...], woa_ref[...])
    yh = _dot(hy_ref[...], woh_ref[...])
    g = 1.0 / (1.0 + jnp.exp(-(gl_ref[...].astype(F32) + bg_ref[...])))
    merged = (g[:, :D_MODEL] * a + g[:, D_MODEL:] * yh).astype(BF16)
    h = x_ref[...] + _dot(merged, wout_ref[...])
    h_ref[...] = h
    hn = _rms(h, fg_ref[...])
    hn_ref[...] = (hn * INV_SQRT2).astype(BF16)
    qp = _dot(hn.astype(BF16), wq_ref[...])
    for c in range(QP_SLOTS):
        qp_ref[c] = qp[:, c * LANES:(c + 1) * LANES].astype(BF16)


def _merge(att, hy, proj, bg, x2, woa, woh, wout, fg, wq):
    T = x2.shape[0]
    half = PEER_KEY_DIM // 2
    row = lambda w: pl.BlockSpec((MERGE_TM, w), lambda i: (i, 0))
    full = lambda a: pl.BlockSpec(a.shape, lambda i: (0, 0))
    return pl.pallas_call(
        _merge_kernel,
        grid=(T // MERGE_TM,),
        in_specs=[row(att.shape[1]), row(hy.shape[1]),
                  pl.BlockSpec((MERGE_TM, 2 * D_MODEL), lambda i: (i, COL_GATE // (2 * D_MODEL))),
                  full(bg), row(D_MODEL), full(woa), full(woh), full(wout), full(fg), full(wq)],
        out_specs=[row(D_MODEL), row(D_MODEL),
                   pl.BlockSpec((QP_SLOTS, MERGE_TM, half), lambda i: (0, i, 0))],
        out_shape=[jax.ShapeDtypeStruct((T, D_MODEL), F32),
                   jax.ShapeDtypeStruct((T, D_MODEL), BF16),
                   jax.ShapeDtypeStruct((QP_SLOTS, T, half), BF16)],
        compiler_params=_cparams(("parallel",)),
        name="merge",
    )(att, hy, proj, bg, x2, woa, woh, wout, fg, wq)


ROUTE_TB = LANES
NEG_INF = float("-inf")
N_CHAINS = 2 * PEER_HEADS


def _sort_pairs(n):
    pairs = []

    def merge(lo, hi, r):
        step = 2 * r
        if step < hi - lo:
            merge(lo, hi, step)
            merge(lo + r, hi, step)
            pairs.extend((i, i + r) for i in range(lo + r, hi - r, step))
        else:
            pairs.append((lo, lo + r))

    def sort(lo, hi):
        if hi > lo:
            mid = lo + (hi - lo) // 2
            sort(lo, mid)
            sort(mid + 1, hi)
            merge(lo, hi, 1)
    sort(0, n - 1)
    return pairs


SUBLANES = 8


def _top_sorted(s):
    n = s.shape[0] // SUBLANES
    v = [s[SUBLANES * i:SUBLANES * (i + 1), :] for i in range(n)]
    for i, j in _sort_pairs(n):
        v[i], v[j] = jnp.maximum(v[i], v[j]), jnp.minimum(v[i], v[j])
    out = []
    for k in range(PEER_TOPK):
        m = jnp.max(v[0], axis=0, keepdims=True)
        out.append(m)
        if k + 1 < PEER_TOPK:
            popped = v[0] == m
            for d in range(PEER_TOPK - 1 - k):
                v[d] = jnp.where(popped, v[d + 1], v[d])
    return jnp.concatenate(out, axis=0)


CAND_ROWS = 72


def _extract_top(work_ref, vals_ref, n_chains):
    slot = lax.broadcasted_iota(jnp.int32, vals_ref.shape[1:], 0)

    def body(k, carry):
        for c in range(n_chains):
            s = work_ref[c]
            m = jnp.max(s, axis=0, keepdims=True)
            work_ref[c] = jnp.where(s == m, NEG_INF, s)
            vals_ref[c] = jnp.where(slot == k, m, vals_ref[c])
        return carry
    lax.fori_loop(0, PEER_TOPK, body, 0)


def _candidates(v1, v2):
    a16 = lax.broadcasted_iota(jnp.int32, v1.shape, 0)
    a8 = a16[0:8]
    return jnp.concatenate([
        v1[0:1] + v2,
        v1[1:2] + v2[0:8], v1[2:3] + v2[0:8], v1[3:4] + v2[0:8],
        jnp.where(a16 >= 4, v1 + v2[0:1], NEG_INF),
        jnp.where(a8 >= 4, v1[0:8] + v2[1:2], NEG_INF),
        jnp.where(a8 == 4, v1[0:8] + v2[2:3], NEG_INF),
    ], axis=0)


def _route_kernel(q_ref, k1_ref, k2_ref, cnt_ref, a_ref, rank_ref, b_ref,
                  s_ref, vals_ref, cand_ref, cvals_ref):
    for h in range(PEER_HEADS):
        for side, k_ref in enumerate((k1_ref, k2_ref)):
            s = _dot_nt(k_ref[h], q_ref[2 * h + side])
            s_ref[2 * h + side] = s
            vals_ref[2 * h + side] = _top_sorted(s)
    for h in range(PEER_HEADS):
        cand_ref[h] = _candidates(vals_ref[2 * h], vals_ref[2 * h + 1])
    cvals_ref[...] = jnp.zeros_like(cvals_ref)
    _extract_top(cand_ref, cvals_ref, PEER_HEADS)

    for h in range(PEER_HEADS):
        v1, v2, cv = vals_ref[2 * h], vals_ref[2 * h + 1], cvals_ref[h]
        s1, s2 = s_ref[2 * h], s_ref[2 * h + 1]
        tau = cv[PEER_TOPK - 1:PEER_TOPK]
        zsum = jnp.sum(jnp.exp(cv - cv[0:1]), axis=0, keepdims=True)
        cnt_sorted = jnp.zeros_like(v1)
        for b in range(PEER_TOPK):
            cnt_sorted = cnt_sorted + jnp.where(v1 + v2[b:b + 1] >= tau, 1.0, 0.0)
        cnt = jnp.zeros_like(s1)
        for a in range(PEER_TOPK):
            cnt = jnp.where(s1 == v1[a:a + 1], cnt_sorted[a:a + 1], cnt)
        rank = jnp.full_like(s2, float(PEER_TOPK))
        for b in reversed(range(PEER_TOPK)):
            rank = jnp.where(s2 >= v2[b:b + 1], float(b), rank)
        cnt_ref[0, h] = cnt * RANK_SCALE
        a_ref[0, h] = jnp.exp(s1 - v1[0:1]) * (GELU_SCALE * WT_SCALE / zsum)
        rank_ref[0, h] = (rank * RANK_SCALE).astype(BF16)
        b_ref[0, h] = jnp.exp(s2 - v2[0:1]).astype(BF16)


def _route(qp, k1, k2):
    T = qp.shape[1]
    half = PEER_KEY_DIM // 2
    blk = pl.BlockSpec((1, PEER_HEADS, N_KEYS, LANES), lambda i: (i, 0, 0, 0))
    shape = lambda dt: jax.ShapeDtypeStruct((T // LANES, PEER_HEADS, N_KEYS, LANES), dt)
    keys = pl.BlockSpec((PEER_HEADS, N_KEYS, half), lambda i: (0, 0, 0))
    return pl.pallas_call(
        _route_kernel,
        grid=(T // ROUTE_TB,),
        in_specs=[pl.BlockSpec((QP_SLOTS, ROUTE_TB, half), lambda i: (0, i, 0)), keys, keys],
        out_specs=[blk, blk, blk, blk],
        out_shape=[shape(F32), shape(F32), shape(BF16), shape(BF16)],
        scratch_shapes=[pltpu.VMEM((N_CHAINS, N_KEYS, ROUTE_TB), F32),
                        pltpu.VMEM((N_CHAINS, PEER_TOPK, ROUTE_TB), F32),
                        pltpu.VMEM((PEER_HEADS, CAND_ROWS, ROUTE_TB), F32),
                        pltpu.VMEM((PEER_HEADS, PEER_TOPK, ROUTE_TB), F32)],
        compiler_params=_cparams(("parallel",)),
        name="route",
    )(qp, k1, k2)


PEER_TB = 512
PEER_TE = 2048
PEER_ROWS = PEER_TE // N_KEYS
PEER_NC = PEER_TB // LANES
INV_SQRT2 = 1.0 / math.sqrt(2.0)
GELU_SCALE = 0.5 * math.sqrt(2.0)
RANK_SCALE = 256.0


def _row_tile(ref, c, h, ii):
    return jnp.broadcast_to(ref[c, h, ii:ii + 1, :], (N_KEYS, LANES)).astype(BF16)


def _peer_kernel(hn_ref, u_ref, vt_ref, cnt_ref, a_ref, rank_ref, b_ref, h_ref, o_ref, y_ref, act_ref, wt_ref):
    e = pl.program_id(1)

    @pl.when(e == 0)
    def _():
        y_ref[...] = jnp.zeros_like(y_ref)

    act = _dot_nt(u_ref[...], hn_ref[...])
    for c in range(PEER_NC):
        act_ref[c] = act[:, c * LANES:(c + 1) * LANES].astype(BF16)

    def chunk(c, carry):
        for ii in range(PEER_ROWS):
            rs = slice(ii * N_KEYS, (ii + 1) * N_KEYS)
            w = jnp.zeros((N_KEYS, LANES), BF16)
            for h in range(PEER_HEADS):
                cnt = _row_tile(cnt_ref, c, h, ii)
                wa = _row_tile(a_ref, c, h, ii)
                sel = jnp.minimum(jnp.maximum(cnt - rank_ref[c, h], 0), b_ref[c, h])
                w = w + sel * wa
            x = act_ref[c, rs, :]
            wt_ref[c, rs, :] = (x * (1 + lax.erf(x)) * w).astype(FP8)
        return carry
    lax.fori_loop(0, PEER_NC, chunk, 0)

    wt = jnp.concatenate([wt_ref[c] for c in range(PEER_NC)], axis=1)
    y_ref[...] += _dot(vt_ref[...], wt)

    @pl.when(e == pl.num_programs(1) - 1)
    def _():
        o_ref[...] = h_ref[...] + y_ref[...].T * (1.0 / (V_SCALE * WT_SCALE))


def _peer(hn, u, vt, cnt, a, rank, b, h):
    T = hn.shape[0]
    rows = pl.BlockSpec((PEER_NC, PEER_HEADS, PEER_ROWS, LANES), lambda t, e: (t, 0, e, 0))
    keys = pl.BlockSpec((PEER_NC, PEER_HEADS, N_KEYS, LANES), lambda t, e: (t, 0, 0, 0))
    tok = pl.BlockSpec((PEER_TB, D_MODEL), lambda t, e: (t, 0))
    return pl.pallas_call(
        _peer_kernel,
        grid=(T // PEER_TB, N_EXPERTS // PEER_TE),
        in_specs=[tok,
                  pl.BlockSpec((PEER_TE, D_MODEL), lambda t, e: (e, 0)),
                  pl.BlockSpec((D_MODEL, PEER_TE), lambda t, e: (0, e)),
                  rows, rows, keys, keys, tok],
        out_specs=tok,
        out_shape=jax.ShapeDtypeStruct((T, D_MODEL), F32),
        scratch_shapes=[pltpu.VMEM((D_MODEL, PEER_TB), F32),
                        pltpu.VMEM((PEER_NC, PEER_TE, LANES), BF16),
                        pltpu.VMEM((PEER_NC, PEER_TE, LANES), FP8)],
        compiler_params=_cparams(("parallel", "arbitrary")),
        name="peer",
    )(hn, u, vt, cnt, a, rank, b, h)


def _regroup_w_in(w):
    o_kv = Q_LORA_RANK
    o_pe = o_kv + KV_LORA_RANK
    o_hy = o_pe + QK_ROPE_DIM
    o_gate = o_hy + 3 * HYENA_WIDTH
    out = jnp.zeros((D_MODEL, PROJ_COLS), BF16)
    out = out.at[:, COL_CQ:COL_CQ + Q_LORA_RANK].set(w[:, :o_kv].astype(BF16))
    out = out.at[:, COL_CKV:COL_CKV + KV_LORA_RANK].set(w[:, o_kv:o_pe].astype(BF16))
    out = out.at[:, COL_KPE + KPE_LANE:COL_KPE + KPE_LANE + QK_ROPE_DIM].set(w[:, o_pe:o_hy].astype(BF16))
    out = out.at[:, COL_HY:COL_HY + 3 * HYENA_WIDTH].set(w[:, o_hy:o_gate].astype(BF16))
    out = out.at[:, COL_GATE:].set(w[:, o_gate:].astype(BF16))
    return out


def _head_slots(w, width):
    k = w.shape[0]
    w3 = w.reshape(k, MLA_HEADS, width).astype(BF16)
    return jnp.zeros((k, MLA_HEADS, HEAD_SLOT), BF16).at[:, :, :width].set(w3).reshape(k, HW)


def kernel(x, attn_norm, w_in, b_gate, q_a_norm, w_uq, kv_a_norm, w_ukv, q_norm, k_norm, w_o_attn, hyena_conv_w, hyena_conv_b, filt_w1, filt_b1, filt_w2, filt_b2, filt_w3, filt_b3, filt_w4, filt_b4, filt_freq, hyena_bias, w_o_hyena, w_out, ffn_norm, peer_w_q, peer_keys1, peer_keys2, expert_u, expert_v):
    B = x.shape[0]
    T = B * SEQ
    x2 = x.reshape(T, D_MODEL)
    bf = lambda a: a.astype(BF16)

    proj = _inproj(x2, attn_norm, _regroup_w_in(w_in[0]))
    kv3 = w_ukv[0].reshape(KV_LORA_RANK, MLA_HEADS, QK_NOPE_DIM + V_HEAD_DIM)
    wuk_p = _head_slots(kv3[:, :, :QK_NOPE_DIM].reshape(KV_LORA_RANK, -1), QK_NOPE_DIM)
    wuv_p = _head_slots(kv3[:, :, QK_NOPE_DIM:].reshape(KV_LORA_RANK, -1), V_HEAD_DIM)
    vone = jnp.zeros((MLA_HEADS, HEAD_SLOT), BF16).at[:, V_HEAD_DIM].set(1.0).reshape(HW, 1)
    wuvt_p = jnp.concatenate([wuv_p.T, _pad2(vone, HW, KV_LORA_RANK)], axis=1)
    wuq_p = _head_slots(w_uq[0], QK_HEAD_DIM)
    wuqr_p = _rope_partner(wuq_p.reshape(Q_LORA_RANK, MLA_HEADS, HEAD_SLOT), axis=2).reshape(Q_LORA_RANK, HW)
    gq = _pad2(q_norm, 1, HEAD_SLOT)
    q, k, vt = _mlaprep(proj, q_a_norm, kv_a_norm, wuq_p, wuqr_p, wuk_p, wuvt_p,
                        gq, _rope_partner(gq, axis=1), _pad2(k_norm, 1, HEAD_SLOT))
    att = _attention(q, k, vt, B).reshape(T, MLA_HEADS * V_HEAD_DIM)

    wf, wft = (jnp.asarray(w).astype(BF16) for w in _dft_tables())
    hsd = _filter(filt_w1[0], filt_b1[0], filt_w2[0], filt_b2[0], filt_w3[0], filt_b3[0], filt_w4[0],
                  filt_b4[0], filt_freq[0])
    kp = _kspec(wf, hsd)
    z, p1, p2 = _hyconv(proj.reshape(B, SEQ, PROJ_COLS), hyena_conv_w[0], hyena_conv_b, hyena_bias)
    ys = _hyfwd(z, wf, kp)
    hy = _hyinv(wft, ys, p1, p2).reshape(T, HYENA_WIDTH)

    h, hn, qp = _merge(att, hy, proj, b_gate, x2, bf(w_o_attn[0]), bf(w_o_hyena[0]), bf(w_out[0]),
                       ffn_norm, bf(peer_w_q[0]))

    cnt, a, rank, b = _route(qp, bf(peer_keys1[0]), bf(peer_keys2[0]))
    out = _peer(hn, bf(expert_u[0]), (expert_v[0] * V_SCALE).astype(FP8).T, cnt, a, rank, b, h)
    return out.reshape(B, SEQ, D_MODEL)
```
